```python
import math
import jax, jax.numpy as jnp
from jax import lax
import numpy as np

D_MODEL = 2048
BATCH = 4
SEQ = 4096
DEPTH = 1

HEAD_DIM = 128
A_Q_HEADS = 8
A_KV_HEADS = 2
B_Q_HEADS = 8
B_KV_HEADS = 2
D_MIX = (A_Q_HEADS + B_Q_HEADS) * HEAD_DIM
QKV_A = (A_Q_HEADS + 2 * A_KV_HEADS) * HEAD_DIM
QKV_B = (B_Q_HEADS + 2 * B_KV_HEADS) * HEAD_DIM
D_IN = QKV_A + QKV_B
BLOCK = 128
WINDOW = 128
GRID_W = 64
ROPE_THETA = 10000.0
ROPE_AXIS_DIM = HEAD_DIM // 2
N_BUCKETS = 32
MAX_DISTANCE = 128
N_EXPERTS = 16
CAPACITY_FACTOR = 2
D_EXPERT = 2048
EPS = 1e-6

kernel_name = "hybrid_axialgqa_swagqa_ecmoe_layer"


def rms_norm(x, g):
    x32 = x.astype(jnp.float32)
    y = x32 * lax.rsqrt(jnp.mean(x32 * x32, axis=-1, keepdims=True) + EPS)
    return (y * g.astype(jnp.float32)).astype(x.dtype)


def modulate(h, shift, scale):
    return h * (1.0 + scale[:, None, :]) + shift[:, None, :]


def axial_rope_tables(S):
    rows = S // GRID_W
    row = jnp.repeat(jnp.arange(rows, dtype=jnp.float32), GRID_W)
    col = jnp.tile(jnp.arange(GRID_W, dtype=jnp.float32), rows)
    inv = 1.0 / (ROPE_THETA ** (jnp.arange(0, ROPE_AXIS_DIM, 2, dtype=jnp.float32) / ROPE_AXIS_DIM))
    ang_r = row[:, None] * inv[None, :]
    ang_c = col[:, None] * inv[None, :]
    return jnp.cos(ang_r), jnp.sin(ang_r), jnp.cos(ang_c), jnp.sin(ang_c)


def rope_half(x, cos, sin):
    n = x.shape[-1] // 2
    x1, x2 = x[..., :n], x[..., n:]
    cos = cos[None, :, None, :]
    sin = sin[None, :, None, :]
    return jnp.concatenate([x1 * cos - x2 * sin, x2 * cos + x1 * sin], axis=-1)


def apply_axial_rope(x, tables):
    cr, sr, cc, sc = tables
    x32 = x.astype(jnp.float32)
    out = jnp.concatenate([rope_half(x32[..., :ROPE_AXIS_DIM], cr, sr),
                           rope_half(x32[..., ROPE_AXIS_DIM:], cc, sc)], axis=-1)
    return out.astype(x.dtype)


def t5_bucket(rel):
    nb = N_BUCKETS // 2
    ret = jnp.where(rel > 0, nb, 0)
    n = jnp.abs(rel)
    max_exact = nb // 2
    nf = jnp.maximum(n, 1).astype(jnp.float32)
    large = max_exact + (jnp.log(nf / max_exact) / math.log(MAX_DISTANCE / max_exact)
                         * (nb - max_exact)).astype(jnp.int32)
    large = jnp.minimum(large, nb - 1)
    return ret + jnp.where(n < max_exact, n, large)


def global_axial_gqa(q, k, v):
    B, S, Hq, d = q.shape
    Hkv = k.shape[2]
    G = Hq // Hkv
    nblk = S // BLOCK
    scale = 1.0 / math.sqrt(d)
    qb = q.reshape(B, nblk, BLOCK, Hkv, G, d).transpose(1, 0, 3, 4, 2, 5)
    kt = k.transpose(0, 2, 1, 3)
    vt = v.transpose(0, 2, 1, 3)

    def one_block(qi):
        s = jnp.einsum('bhgqd,bhkd->bhgqk', qi, kt).astype(jnp.float32) * scale
        p = jax.nn.softmax(s, axis=-1)
        return jnp.einsum('bhgqk,bhkd->bhgqd', p.astype(vt.dtype), vt)

    o = lax.map(one_block, qb)
    return o.transpose(1, 0, 4, 2, 3, 5).reshape(B, S, Hq * d)


def windowed_sink_gqa(q, k, v, rel_bias, sink):
    B, S, Hq, d = q.shape
    Hkv = k.shape[2]
    G = Hq // Hkv
    nblk = S // BLOCK
    scale = 1.0 / math.sqrt(d)
    qb = q.reshape(B, nblk, BLOCK, Hkv, G, d).transpose(0, 3, 4, 1, 2, 5)

    def band(t):
        tp = jnp.pad(t.transpose(0, 2, 1, 3), ((0, 0), (0, 0), (BLOCK, BLOCK), (0, 0)))
        tb = tp.reshape(B, Hkv, nblk + 2, BLOCK, d)
        return jnp.concatenate([tb[:, :, :-2], tb[:, :, 1:-1], tb[:, :, 2:]], axis=3)

    kw, vw = band(k), band(v)
    m = jnp.arange(3 * BLOCK)
    j = jnp.arange(BLOCK)
    rel = m[None, :] - BLOCK - j[:, None]
    kpos = jnp.arange(nblk)[:, None, None] * BLOCK - BLOCK + m[None, None, :]
    valid = (jnp.abs(rel) <= WINDOW)[None] & (kpos >= 0) & (kpos < S)
    bias = rel_bias.astype(jnp.float32)[t5_bucket(rel)]
    bias = bias.transpose(2, 0, 1).reshape(Hkv, G, BLOCK, 3 * BLOCK)

    s = jnp.einsum('bhgnqd,bhnkd->bhgnqk', qb, kw).astype(jnp.float32) * scale
    s = s + bias[None, :, :, None, :, :]
    s = jnp.where(valid[None, None, None], s, -1e30)
    sink_col = jnp.broadcast_to(sink.astype(jnp.float32).reshape(1, Hkv, G, 1, 1, 1),
                                s.shape[:-1] + (1,))
    p = jax.nn.softmax(jnp.concatenate([s, sink_col], axis=-1), axis=-1)[..., :-1]
    o = jnp.einsum('bhgnqk,bhnkd->bhgnqd', p.astype(vw.dtype), vw)
    return o.transpose(0, 3, 4, 1, 2, 5).reshape(B, S, Hq * d)


def expert_choice_moe(h, w_router, w_gate, w_up, w_down):
    B, S, D = h.shape
    C = CAPACITY_FACTOR * S // N_EXPERTS
    logits = jnp.einsum('bsd,de->bse', h, w_router).astype(jnp.float32)
    aff = jax.nn.softmax(logits, axis=-1)
    g, idx = lax.top_k(aff.transpose(0, 2, 1), C)
    xg = jax.vmap(lambda hb, ib: hb[ib])(h, idx)
    a = jnp.einsum('becd,edf->becf', xg, w_gate)
    u = jnp.einsum('becd,edf->becf', xg, w_up)
    y = jnp.einsum('becf,efd->becd', jax.nn.silu(a) * u, w_down)
    y = y * g[..., None].astype(y.dtype)
    return jax.vmap(lambda yb, ib: jnp.zeros((S, D), yb.dtype).at[ib.reshape(-1)].add(yb.reshape(-1, D)))(y, idx)


def setup_inputs(seed: int = 0) -> dict:
    key = jax.random.key(seed)
    ks = jax.random.split(key, 20)
    f32 = jnp.float32
    nrm = lambda k, shape, s: jax.random.normal(k, shape, f32) * s
    return {
        "x": nrm(ks[0], (BATCH, SEQ, D_MODEL), 1.0),
        "c": nrm(ks[1], (BATCH, D_MODEL), 1.0),
        "w_ada": nrm(ks[2], (D_MODEL, 6 * D_MODEL), 0.5 * D_MODEL ** -0.5),
        "b_ada": nrm(ks[3], (6 * D_MODEL,), 0.02),
        "g_norm1": 1.0 + nrm(ks[4], (D_MODEL,), 0.02),
        "w_in": nrm(ks[5], (D_MODEL, D_IN), D_MODEL ** -0.5),
        "qn_a": 1.0 + nrm(ks[6], (HEAD_DIM,), 0.02),
        "kn_a": 1.0 + nrm(ks[7], (HEAD_DIM,), 0.02),
        "qn_b": 1.0 + nrm(ks[8], (HEAD_DIM,), 0.02),
        "kn_b": 1.0 + nrm(ks[9], (HEAD_DIM,), 0.02),
        "sink_b": nrm(ks[10], (B_Q_HEADS,), 0.5),
        "rel_bias": nrm(ks[11], (N_BUCKETS, B_Q_HEADS), 0.1),
        "w_out": nrm(ks[12], (D_MIX, D_MODEL), D_MIX ** -0.5),
        "g_norm2": 1.0 + nrm(ks[13], (D_MODEL,), 0.02),
        "w_router": nrm(ks[14], (D_MODEL, N_EXPERTS), D_MODEL ** -0.5),
        "w_gate": nrm(ks[15], (N_EXPERTS, D_MODEL, D_EXPERT), D_MODEL ** -0.5),
        "w_up": nrm(ks[16], (N_EXPERTS, D_MODEL, D_EXPERT), D_MODEL ** -0.5),
        "w_down": nrm(ks[17], (N_EXPERTS, D_EXPERT, D_MODEL), D_EXPERT ** -0.5),
    }


def reference(x, c, w_ada, b_ada, g_norm1, w_in, qn_a, kn_a, qn_b, kn_b, sink_b,
              rel_bias, w_out, g_norm2, w_router, w_gate, w_up, w_down):
    B, S, D = x.shape
    tables = axial_rope_tables(S)
    oq = A_Q_HEADS * HEAD_DIM
    okv = A_KV_HEADS * HEAD_DIM
    bq = B_Q_HEADS * HEAD_DIM
    bkv = B_KV_HEADS * HEAD_DIM
    for _ in range(DEPTH):
        mod = jnp.einsum('bd,de->be', jax.nn.silu(c), w_ada) + b_ada
        sh1, sc1, gt1, sh2, sc2, gt2 = jnp.split(mod, 6, axis=-1)

        h = modulate(rms_norm(x, g_norm1), sh1, sc1)
        z = jnp.einsum('bsd,de->bse', h, w_in)
        qa = z[..., :oq].reshape(B, S, A_Q_HEADS, HEAD_DIM)
        ka = z[..., oq:oq + okv].reshape(B, S, A_KV_HEADS, HEAD_DIM)
        va = z[..., oq + okv:QKV_A].reshape(B, S, A_KV_HEADS, HEAD_DIM)
        zb = z[..., QKV_A:]
        qb = zb[..., :bq].reshape(B, S, B_Q_HEADS, HEAD_DIM)
        kb = zb[..., bq:bq + bkv].reshape(B, S, B_KV_HEADS, HEAD_DIM)
        vb = zb[..., bq + bkv:].reshape(B, S, B_KV_HEADS, HEAD_DIM)

        qa = apply_axial_rope(rms_norm(qa, qn_a), tables)
        ka = apply_axial_rope(rms_norm(ka, kn_a), tables)
        oa = global_axial_gqa(qa, ka, va)

        qb = rms_norm(qb, qn_b)
        kb = rms_norm(kb, kn_b)
        ob = windowed_sink_gqa(qb, kb, vb, rel_bias, sink_b)

        mix = jnp.einsum('bse,ed->bsd', jnp.concatenate([oa, ob], axis=-1), w_out)
        x = x + gt1[:, None, :] * mix

        h2 = modulate(rms_norm(x, g_norm2), sh2, sc2)
        x = x + gt2[:, None, :] * expert_choice_moe(h2, w_router, w_gate, w_up, w_down)
    return x
```

```python
import functools
import math

import jax
import jax.numpy as jnp
from jax import lax
from jax.experimental import pallas as pl
from jax.experimental.pallas import tpu as pltpu

HEAD_DIM = 128
A_Q_HEADS = 8
A_KV_HEADS = 2
B_Q_HEADS = 8
B_KV_HEADS = 2
GQA_GROUP = 4
BLOCK = 128
WINDOW = 128
GRID_W = 64
ROPE_THETA = 10000.0
ROPE_AXIS_DIM = HEAD_DIM // 2
N_BUCKETS = 32
MAX_DISTANCE = 128
N_EXPERTS = 16
CAPACITY_FACTOR = 2
EPS = 1e-6
NEG_BIG = -1e30

LANES = 128
TOK_PER_ROW = LANES // N_EXPERTS
VMEM_LIMIT = 56 * 1024 * 1024

F32 = jnp.float32
BF16 = jnp.bfloat16


def _cparams(sem):
    return pltpu.CompilerParams(dimension_semantics=sem, vmem_limit_bytes=VMEM_LIMIT)


def _ada_kernel(c_ref, w_ref, b_ref, o_ref):
    c = c_ref[...]
    s = c * jax.nn.sigmoid(c)
    o_ref[...] = jnp.dot(s.astype(BF16), w_ref[...].astype(BF16),
                         preferred_element_type=F32) + b_ref[...]


def _ada_mod(c, w_ada, b_ada):
    B, D = c.shape
    N = w_ada.shape[1]
    rows = 8
    cp = jnp.zeros((rows, D), F32).at[:B].set(c)
    tn = 1024 if N % 1024 == 0 else 512
    assert N % tn == 0
    out = pl.pallas_call(
        _ada_kernel,
        out_shape=jax.ShapeDtypeStruct((rows, N), F32),
        grid=(N // tn,),
        in_specs=[pl.BlockSpec((rows, D), lambda j: (0, 0)),
                  pl.BlockSpec((D, tn), lambda j: (0, j)),
                  pl.BlockSpec((1, tn), lambda j: (0, j))],
        out_specs=pl.BlockSpec((rows, tn), lambda j: (0, j)),
        compiler_params=_cparams(("arbitrary",)),
        name="ada_mod",
    )(cp, w_ada, b_ada.reshape(1, N))
    return out[:B]


def _head_norm(z, gain):
    ms = jnp.mean(z * z, axis=-1, keepdims=True)
    return z * lax.rsqrt(ms + EPS) * gain


def _qkv_kernel(x_ref, g1_ref, sc_ref, sh_ref, w_ref, qna_ref, kna_ref, qnb_ref, knb_ref,
                cos_ref, sin_ref, qa_ref, ka_ref, va_ref, qb_ref, kb_ref, vb_ref):
    x = x_ref[0]
    ms = jnp.mean(x * x, axis=-1, keepdims=True)
    h = (x * lax.rsqrt(ms + EPS) * g1_ref[...]) * (1.0 + sc_ref[0]) + sh_ref[0]
    z = jnp.dot(h.astype(BF16), w_ref[...], preferred_element_type=F32)

    cos = cos_ref[...]
    sin = sin_ref[...]
    lane = lax.broadcasted_iota(jnp.int32, cos.shape, 1)
    first_half = (lane % (ROPE_AXIS_DIM)) < (ROPE_AXIS_DIM // 2)
    scale = 1.0 / math.sqrt(HEAD_DIM)

    def rope(v):
        partner = jnp.where(first_half,
                            pltpu.roll(v, HEAD_DIM - ROPE_AXIS_DIM // 2, 1),
                            pltpu.roll(v, ROPE_AXIS_DIM // 2, 1))
        return v * cos + partner * sin

    def head(col):
        return z[:, col * HEAD_DIM:(col + 1) * HEAD_DIM]

    col = 0
    for hh in range(A_Q_HEADS):
        qa_ref[0, hh] = (rope(_head_norm(head(col), qna_ref[...])) * scale).astype(BF16)
        col += 1
    for hh in range(A_KV_HEADS):
        ka_ref[0, hh] = rope(_head_norm(head(col), kna_ref[...])).astype(BF16)
        col += 1
    for hh in range(A_KV_HEADS):
        va_ref[0, hh] = head(col).astype(BF16)
        col += 1
    for hh in range(B_Q_HEADS):
        qb_ref[0, hh] = (_head_norm(head(col), qnb_ref[...]) * scale).astype(BF16)
        col += 1
    for hh in range(B_KV_HEADS):
        kb_ref[0, hh] = _head_norm(head(col), knb_ref[...]).astype(BF16)
        col += 1
    for hh in range(B_KV_HEADS):
        vb_ref[0, hh] = head(col).astype(BF16)
        col += 1


def _rope_tables(S):
    rows = S // GRID_W
    row = jnp.repeat(jnp.arange(rows, dtype=F32), GRID_W)
    col = jnp.tile(jnp.arange(GRID_W, dtype=F32), rows)
    inv = 1.0 / (ROPE_THETA ** (jnp.arange(0, ROPE_AXIS_DIM, 2, dtype=F32) / ROPE_AXIS_DIM))
    ang_r = row[:, None] * inv[None, :]
    ang_c = col[:, None] * inv[None, :]
    cr, sr, cc, sc = jnp.cos(ang_r), jnp.sin(ang_r), jnp.cos(ang_c), jnp.sin(ang_c)
    cos = jnp.concatenate([cr, cr, cc, cc], axis=-1)
    sin = jnp.concatenate([-sr, sr, -sc, sc], axis=-1)
    return cos, sin


def _qkv_proj(x, g1, sc1, sh1, w_in_bf, qn_a, kn_a, qn_b, kn_b, tm):
    B, S, D = x.shape
    d_in = w_in_bf.shape[1]
    cos, sin = _rope_tables(S)
    vec = lambda v: v.reshape(1, -1).astype(F32)
    const = lambda shape: pl.BlockSpec(shape, lambda b, i: (0,) * len(shape))
    head_out = lambda nh: pl.BlockSpec((1, nh, tm, HEAD_DIM), lambda b, i: (b, 0, i, 0))
    head_shape = lambda nh: jax.ShapeDtypeStruct((B, nh, S, HEAD_DIM), BF16)
    return pl.pallas_call(
        _qkv_kernel,
        out_shape=(head_shape(A_Q_HEADS), head_shape(A_KV_HEADS), head_shape(A_KV_HEADS),
                   head_shape(B_Q_HEADS), head_shape(B_KV_HEADS), head_shape(B_KV_HEADS)),
        grid=(B, S // tm),
        in_specs=[pl.BlockSpec((1, tm, D), lambda b, i: (b, i, 0)),
                  const((1, D)),
                  pl.BlockSpec((1, 1, D), lambda b, i: (b, 0, 0)),
                  pl.BlockSpec((1, 1, D), lambda b, i: (b, 0, 0)),
                  pl.BlockSpec((D, d_in), lambda b, i: (0, 0), pipeline_mode=pl.Buffered(1)),
                  const((1, HEAD_DIM)), const((1, HEAD_DIM)), const((1, HEAD_DIM)), const((1, HEAD_DIM)),
                  pl.BlockSpec((tm, HEAD_DIM), lambda b, i: (i, 0)),
                  pl.BlockSpec((tm, HEAD_DIM), lambda b, i: (i, 0))],
        out_specs=(head_out(A_Q_HEADS), head_out(A_KV_HEADS), head_out(A_KV_HEADS),
                   head_out(B_Q_HEADS), head_out(B_KV_HEADS), head_out(B_KV_HEADS)),
        compiler_params=_cparams(("parallel", "parallel")),
        name="qkv_proj",
    )(x, vec(g1), sc1, sh1, w_in_bf, vec(qn_a), vec(kn_a), vec(qn_b), vec(kn_b), cos, sin)


def _attn_a_kernel(q_ref, k_ref, v_ref, o_ref, *, tk):
    G, tq, d = q_ref.shape[1:]
    S = k_ref.shape[2]
    q = q_ref[0].reshape(G * tq, d)

    def body(j, carry):
        m, l, acc = carry
        k0 = pl.multiple_of(j * tk, tk)
        k = k_ref[0, 0, pl.ds(k0, tk), :]
        v = v_ref[0, 0, pl.ds(k0, tk), :]
        s = lax.dot_general(q, k, (((1,), (1,)), ((), ())), preferred_element_type=F32)
        m_new = jnp.maximum(m, jnp.max(s, axis=-1, keepdims=True))
        alpha = jnp.exp(m - m_new)
        p = jnp.exp(s - m_new)
        l = alpha * l + jnp.sum(p, axis=-1, keepdims=True)
        acc = alpha * acc + jnp.dot(p.astype(BF16), v, preferred_element_type=F32)
        return m_new, l, acc

    m0 = jnp.full((G * tq, 1), -jnp.inf, F32)
    l0 = jnp.zeros((G * tq, 1), F32)
    a0 = jnp.zeros((G * tq, d), F32)
    _, l, acc = lax.fori_loop(0, S // tk, body, (m0, l0, a0))
    out = (acc / l).astype(o_ref.dtype)
    for g in range(G):
        o_ref[0, :, g * d:(g + 1) * d] = out[g * tq:(g + 1) * tq]


def _attn_a(q, k, v, tq, tk):
    B, Hq, S, d = q.shape
    Hkv = k.shape[1]
    G = Hq // Hkv
    return pl.pallas_call(
        functools.partial(_attn_a_kernel, tk=tk),
        out_shape=jax.ShapeDtypeStruct((B, S, Hq * d), BF16),
        grid=(B, Hkv, S // tq),
        in_specs=[pl.BlockSpec((1, G, tq, d), lambda b, h, i: (b, h, i, 0)),
                  pl.BlockSpec((1, 1, S, d), lambda b, h, i: (b, h, 0, 0)),
                  pl.BlockSpec((1, 1, S, d), lambda b, h, i: (b, h, 0, 0))],
        out_specs=pl.BlockSpec((1, tq, G * d), lambda b, h, i: (b, i, h)),
        compiler_params=_cparams(("parallel", "parallel", "arbitrary")),
        name="attn_global",
    )(q, k, v)


def _attn_b_kernel(q_ref, k_ref, v_ref, bias_ref, sink_ref, o_ref):
    G, S, d = q_ref.shape[1:]
    kw = 3 * BLOCK
    bias = bias_ref[...]
    sink = sink_ref[...]
    koff = lax.broadcasted_iota(jnp.int32, (1, 1, kw), 2) - BLOCK

    def body(n, carry):
        r0 = pl.multiple_of(n * BLOCK, BLOCK)
        q = q_ref[0, :, pl.ds(r0, BLOCK), :].reshape(G * BLOCK, d)
        k = k_ref[0, 0, pl.ds(r0, kw), :]
        v = v_ref[0, 0, pl.ds(r0, kw), :]
        s = lax.dot_general(q, k, (((1,), (1,)), ((), ())), preferred_element_type=F32)
        s = s.reshape(G, BLOCK, kw) + bias
        kpos = koff + r0
        s = jnp.where((kpos >= 0) & (kpos < S), s, NEG_BIG)
        m = jnp.maximum(jnp.max(s, axis=-1, keepdims=True), sink)
        p = jnp.exp(s - m)
        denom = jnp.sum(p, axis=-1, keepdims=True) + jnp.exp(sink - m)
        o = jnp.dot(p.reshape(G * BLOCK, kw).astype(BF16), v, preferred_element_type=F32)
        o = (o.reshape(G, BLOCK, d) / denom).astype(o_ref.dtype)
        for g in range(G):
            o_ref[0, pl.ds(r0, BLOCK), g * d:(g + 1) * d] = o[g]
        return carry

    lax.fori_loop(0, S // BLOCK, body, 0)


def _t5_bucket(rel):
    nb = N_BUCKETS // 2
    ret = jnp.where(rel > 0, nb, 0)
    n = jnp.abs(rel)
    max_exact = nb // 2
    nf = jnp.maximum(n, 1).astype(F32)
    large = max_exact + (jnp.log(nf / max_exact) / math.log(MAX_DISTANCE / max_exact)
                         * (nb - max_exact)).astype(jnp.int32)
    large = jnp.minimum(large, nb - 1)
    return ret + jnp.where(n < max_exact, n, large)


def _window_bias(rel_bias):
    m = jnp.arange(3 * BLOCK)
    j = jnp.arange(BLOCK)
    rel = m[None, :] - BLOCK - j[:, None]
    bias = rel_bias.astype(F32)[_t5_bucket(rel)]
    bias = jnp.where((jnp.abs(rel) <= WINDOW)[..., None], bias, NEG_BIG)
    return bias.transpose(2, 0, 1)


def _attn_b(q, k, v, rel_bias, sink):
    B, Hq, S, d = q.shape
    Hkv = k.shape[1]
    G = Hq // Hkv
    pad = ((0, 0), (0, 0), (BLOCK, BLOCK), (0, 0))
    kp = jnp.pad(k, pad)
    vp = jnp.pad(v, pad)
    bias = _window_bias(rel_bias)
    return pl.pallas_call(
        _attn_b_kernel,
        out_shape=jax.ShapeDtypeStruct((B, S, Hq * d), BF16),
        grid=(B, Hkv),
        in_specs=[pl.BlockSpec((1, G, S, d), lambda b, h: (b, h, 0, 0)),
                  pl.BlockSpec((1, 1, S + 2 * BLOCK, d), lambda b, h: (b, h, 0, 0)),
                  pl.BlockSpec((1, 1, S + 2 * BLOCK, d), lambda b, h: (b, h, 0, 0)),
                  pl.BlockSpec((G, BLOCK, 3 * BLOCK), lambda b, h: (h, 0, 0)),
                  pl.BlockSpec((G, 1, 1), lambda b, h: (h, 0, 0))],
        out_specs=pl.BlockSpec((1, S, G * d), lambda b, h: (b, 0, h)),
        compiler_params=_cparams(("parallel", "parallel")),
        name="attn_window",
    )(q, kp, vp, bias, sink.astype(F32).reshape(Hq, 1, 1))


def _outproj_kernel(oa_ref, ob_ref, x_ref, w_ref, gt_ref, g2_ref, sc_ref, sh_ref, wr_ref,
                    x1_ref, h2_ref, aff_ref):
    half = oa_ref.shape[2]
    mix = jnp.dot(oa_ref[0], w_ref[:half, :], preferred_element_type=F32)
    mix = mix + jnp.dot(ob_ref[0], w_ref[half:, :], preferred_element_type=F32)
    x1 = x_ref[0] + gt_ref[0] * mix
    x1_ref[0] = x1
    ms = jnp.mean(x1 * x1, axis=-1, keepdims=True)
    h2 = (x1 * lax.rsqrt(ms + EPS) * g2_ref[...]) * (1.0 + sc_ref[0]) + sh_ref[0]
    h2_ref[0] = h2
    logits = jnp.dot(h2.astype(BF16), wr_ref[...], preferred_element_type=F32)
    e = jnp.exp(logits - jnp.max(logits, axis=-1, keepdims=True))
    aff_ref[0] = e / jnp.sum(e, axis=-1, keepdims=True)


def _outproj(oa, ob, x, w_out_bf, gt1, g2, sc2, sh2, wr_bf, tm):
    B, S, D = x.shape
    half = oa.shape[2]
    E = wr_bf.shape[1]
    const = lambda shape: pl.BlockSpec(shape, lambda b, i: (0,) * len(shape))
    per_b = pl.BlockSpec((1, 1, D), lambda b, i: (b, 0, 0))
    return pl.pallas_call(
        _outproj_kernel,
        out_shape=(jax.ShapeDtypeStruct((B, S, D), F32),
                   jax.ShapeDtypeStruct((B, S, D), F32),
                   jax.ShapeDtypeStruct((B, S, E), F32)),
        grid=(B, S // tm),
        in_specs=[pl.BlockSpec((1, tm, half), lambda b, i: (b, i, 0)),
                  pl.BlockSpec((1, tm, half), lambda b, i: (b, i, 0)),
                  pl.BlockSpec((1, tm, D), lambda b, i: (b, i, 0)),
                  pl.BlockSpec((2 * half, D), lambda b, i: (0, 0), pipeline_mode=pl.Buffered(1)),
                  per_b, const((1, D)), per_b, per_b,
                  const((D, E))],
        out_specs=(pl.BlockSpec((1, tm, D), lambda b, i: (b, i, 0)),
                   pl.BlockSpec((1, tm, D), lambda b, i: (b, i, 0)),
                   pl.BlockSpec((1, tm, E), lambda b, i: (b, i, 0))),
        compiler_params=_cparams(("parallel", "parallel")),
        name="outproj_router",
    )(oa, ob, x, w_out_bf, gt1, g2.reshape(1, D).astype(F32), sc2, sh2, wr_bf)


def _group_allreduce(v, op):
    shift = N_EXPERTS
    while shift < LANES:
        v = op(v, pltpu.roll(v, shift, 1))
        shift *= 2
    return v


def _select_kernel(aff_ref, idx_ref, rot_ref, *, cap, n_bisect):
    B, R, L = aff_ref.shape
    a = aff_ref[...]
    capf = jnp.float32(cap)

    def count(mask):
        return _group_allreduce(jnp.sum(jnp.where(mask, 1.0, 0.0), axis=1), jnp.add)

    def bis(_, carry):
        lo, hi = carry
        mid = 0.5 * (lo + hi)
        ge = count(a > mid[:, None, :]) >= capf
        return jnp.where(ge, mid, lo), jnp.where(ge, hi, mid)

    lo0 = jnp.full((B, L), -1.0, F32)
    hi0 = jnp.full((B, L), 1.0, F32)
    lo, _ = lax.fori_loop(0, n_bisect, bis, (lo0, hi0))
    tau = jnp.min(jnp.where(a > lo[:, None, :], a, jnp.inf), axis=1)
    tau = _group_allreduce(tau, jnp.minimum)[:, None, :]

    lane_r = lax.broadcasted_iota(jnp.int32, (L, L), 0)
    lane_c = lax.broadcasted_iota(jnp.int32, (L, L), 1)
    same_e = (lane_r % N_EXPERTS) == (lane_c % N_EXPERTS)
    g_incl = jnp.where(same_e & (lane_r // N_EXPERTS <= lane_c // N_EXPERTS), 1.0, 0.0).astype(BF16)
    g_full = jnp.where(same_e, 1.0, 0.0).astype(BF16)
    row_r = lax.broadcasted_iota(jnp.int32, (R, R), 0)
    row_c = lax.broadcasted_iota(jnp.int32, (R, R), 1)
    l_strict = jnp.where(row_c < row_r, 1.0, 0.0).astype(BF16)

    def prefix_incl(mask):
        ones = jnp.where(mask, 1.0, 0.0)
        outs = []
        for b in range(B):
            mb = ones[b].astype(BF16)
            within = jnp.dot(mb, g_incl, preferred_element_type=F32)
            tot = jnp.dot(mb, g_full, preferred_element_type=F32)
            before = jnp.dot(l_strict, tot.astype(BF16), preferred_element_type=F32)
            outs.append((within + before)[None])
        return jnp.concatenate(outs, axis=0)

    gt = a > tau
    eq = a == tau
    need = (capf - count(gt))[:, None, :]
    eq_before = prefix_incl(eq) - jnp.where(eq, 1.0, 0.0)
    sel = gt | (eq & (eq_before < need))
    rank = prefix_incl(sel)

    rank2 = rank.reshape(B * R, L)
    for rho in range(TOK_PER_ROW):
        rot_ref[rho] = rank2 if rho == 0 else pltpu.roll(rank2, rho * N_EXPERTS, 1)
    slot_in_row = (lax.broadcasted_iota(jnp.int32, (1, L), 1) // N_EXPERTS).astype(F32)

    def slots(i, carry):
        cvec = slot_in_row + jnp.asarray(i * TOK_PER_ROW, F32)
        tot = jnp.zeros((B, L), F32)
        for rho in range(TOK_PER_ROW):
            le = jnp.where(rot_ref[rho] <= cvec, 1.0, 0.0)
            tot = tot + jnp.sum(le.reshape(B, R, L), axis=1)
        idx_ref[:, pl.ds(i, 1), :] = tot.astype(jnp.int32)[:, None, :]
        return carry

    lax.fori_loop(0, cap // TOK_PER_ROW, slots, 0)


def _select(aff, cap):
    B, S, E = aff.shape
    R = S // TOK_PER_ROW
    packed = aff.reshape(B, R, LANES)
    idx = pl.pallas_call(
        functools.partial(_select_kernel, cap=cap, n_bisect=160),
        out_shape=jax.ShapeDtypeStruct((B, cap // TOK_PER_ROW, LANES), jnp.int32),
        grid=(1,),
        in_specs=[pl.BlockSpec((B, R, LANES), lambda i: (0, 0, 0))],
        out_specs=pl.BlockSpec((B, cap // TOK_PER_ROW, LANES), lambda i: (0, 0, 0)),
        scratch_shapes=[pltpu.VMEM((TOK_PER_ROW, B * R, LANES), F32)],
        compiler_params=_cparams(("arbitrary",)),
        name="expert_select",
    )(packed)
    return idx.reshape(B, cap, E).transpose(0, 2, 1)


def _expert_kernel(idx_ref, h2_hbm, acc_hbm, wg_ref, wu_ref, wd_ref, wr_ref, gt_ref,
                   out_hbm, xb_ref, y_ref, g_ref, sem, *, cap, tn):
    del acc_hbm
    e = pl.program_id(0)
    hf = pl.program_id(1)
    f = pl.program_id(2)
    nf = pl.num_programs(2)
    M, D = y_ref.shape
    base = (e * pl.num_programs(1) + hf) * M

    def row_copies(src_hbm, to_hbm, action):
        def body(s, carry):
            row = idx_ref[base + s]
            if to_hbm:
                cp = pltpu.make_async_copy(y_ref.at[pl.ds(s, 1)], src_hbm.at[pl.ds(row, 1)], sem)
            else:
                cp = pltpu.make_async_copy(src_hbm.at[pl.ds(row, 1)], y_ref.at[pl.ds(s, 1)], sem)
            getattr(cp, action)()
            return carry
        lax.fori_loop(0, M, body, 0)

    @pl.when(f == 0)
    def _():
        row_copies(h2_hbm, False, "start")
        row_copies(h2_hbm, False, "wait")
        xb = y_ref[...].astype(BF16)
        xb_ref[...] = xb
        row_copies(out_hbm, False, "start")
        logits = jnp.dot(xb, wr_ref[...], preferred_element_type=F32)
        ex = jnp.exp(logits - jnp.max(logits, axis=-1, keepdims=True))
        aff = ex / jnp.sum(ex, axis=-1, keepdims=True)
        lane = lax.broadcasted_iota(jnp.int32, aff.shape, 1)
        g_ref[...] = jnp.sum(jnp.where(lane == e, aff, 0.0), axis=-1, keepdims=True)
        row_copies(out_hbm, False, "wait")

    xb = xb_ref[...]
    a = jnp.dot(xb, wg_ref[0].astype(BF16), preferred_element_type=F32)
    u = jnp.dot(xb, wu_ref[0].astype(BF16), preferred_element_type=F32)
    hmid = ((a * jax.nn.sigmoid(a)) * u).astype(BF16)
    g = g_ref[...]
    nb = gt_ref.shape[0]
    for n0 in range(0, D, tn):
        yc = jnp.dot(hmid, wd_ref[0, :, n0:n0 + tn].astype(BF16), preferred_element_type=F32)
        yc = yc * g
        for b in range(nb):
            r0, r1 = b * cap, (b + 1) * cap
            y_ref[r0:r1, n0:n0 + tn] += gt_ref[b, :, n0:n0 + tn] * yc[r0:r1]

    @pl.when(f == nf - 1)
    def _():
        row_copies(out_hbm, True, "start")
        row_copies(out_hbm, True, "wait")


def _experts(idx_rows, h2, x1, w_gate, w_up, w_down, wr_bf, gt2, cap, n_half, tf, tn):
    N, D = h2.shape
    E, _, F = w_gate.shape
    B = gt2.shape[0]
    bh = B // n_half
    M = bh * cap
    grid_spec = pltpu.PrefetchScalarGridSpec(
        num_scalar_prefetch=1,
        grid=(E, n_half, F // tf),
        in_specs=[pl.BlockSpec(memory_space=pl.ANY),
                  pl.BlockSpec(memory_space=pl.ANY),
                  pl.BlockSpec((1, D, tf), lambda e, h, f, idx: (e, 0, f)),
                  pl.BlockSpec((1, D, tf), lambda e, h, f, idx: (e, 0, f)),
                  pl.BlockSpec((1, tf, D), lambda e, h, f, idx: (e, f, 0)),
                  pl.BlockSpec((D, E), lambda e, h, f, idx: (0, 0)),
                  pl.BlockSpec((bh, 1, D), lambda e, h, f, idx: (h, 0, 0))],
        out_specs=pl.BlockSpec(memory_space=pl.ANY),
        scratch_shapes=[pltpu.VMEM((M, D), BF16),
                        pltpu.VMEM((M, D), F32),
                        pltpu.VMEM((M, 1), F32),
                        pltpu.SemaphoreType.DMA],
    )
    return pl.pallas_call(
        functools.partial(_expert_kernel, cap=cap, tn=tn),
        out_shape=jax.ShapeDtypeStruct((N, D), F32),
        grid_spec=grid_spec,
        input_output_aliases={2: 0},
        compiler_params=_cparams(("arbitrary", "arbitrary", "arbitrary")),
        name="moe_experts",
    )(idx_rows, h2, x1, w_gate, w_up, w_down, wr_bf, gt2)


def kernel(x, c, w_ada, b_ada, g_norm1, w_in, qn_a, kn_a, qn_b, kn_b, sink_b, rel_bias,
           w_out, g_norm2, w_router, w_gate, w_up, w_down):
    B, S, D = x.shape
    E = w_router.shape[1]
    cap = CAPACITY_FACTOR * S // E

    mod = _ada_mod(c, w_ada, b_ada).reshape(B, 6, 1, D)
    sh1, sc1, gt1, sh2, sc2, gt2 = (mod[:, i] for i in range(6))

    tm = min(512, S)
    qa, ka, va, qb, kb, vb = _qkv_proj(x, g_norm1, sc1, sh1, w_in.astype(BF16),
                                       qn_a, kn_a, qn_b, kn_b, tm)
    oa = _attn_a(qa, ka, va, tq=min(256, S), tk=min(512, S))
    ob = _attn_b(qb, kb, vb, rel_bias, sink_b)

    x1, h2, aff = _outproj(oa, ob, x, w_out.astype(BF16), gt1, g_norm2, sc2, sh2,
                           w_router.astype(BF16), tm)

    idx = _select(aff, cap)
    rows = idx + (jnp.arange(B, dtype=jnp.int32) * S)[:, None, None]
    rows = rows.transpose(1, 0, 2).reshape(-1)

    out = _experts(rows, h2.reshape(B * S, D), x1.reshape(B * S, D), w_gate, w_up, w_down,
                   w_router.astype(BF16), gt2, cap, n_half=2 if B % 2 == 0 else 1,
                   tf=256, tn=512)
    return out.reshape(B, S, D)
```

```python
import functools
import math

import jax
import jax.numpy as jnp
from jax import lax
from jax.experimental import pallas as pl
from jax.experimental.pallas import tpu as pltpu

HEAD_DIM = 128
A_Q_HEADS = 8
A_KV_HEADS = 2
B_Q_HEADS = 8
B_KV_HEADS = 2
GQA_GROUP = 4
BLOCK = 128
WINDOW = 128
GRID_W = 64
ROPE_THETA = 10000.0
ROPE_AXIS_DIM = HEAD_DIM // 2
N_BUCKETS = 32
MAX_DISTANCE = 128
N_EXPERTS = 16
CAPACITY_FACTOR = 2
EPS = 1e-6
NEG_BIG = -1e30

LANES = 128
TOK_PER_ROW = LANES // N_EXPERTS
VMEM_LIMIT = 56 * 1024 * 1024
SUBLANES = 8

F32 = jnp.float32
BF16 = jnp.bfloat16


def _cparams(sem):
    return pltpu.CompilerParams(dimension_semantics=sem, vmem_limit_bytes=VMEM_LIMIT)


def _ada_kernel(c_ref, w_ref, b_ref, o_ref):
    c = c_ref[...]
    s = c * jax.nn.sigmoid(c)
    o_ref[...] = jnp.dot(s.astype(BF16), w_ref[...].astype(BF16),
                         preferred_element_type=F32) + b_ref[...]


def _ada_mod(c, w_ada, b_ada):
    B, D = c.shape
    N = w_ada.shape[1]
    rows = 8
    cp = jnp.zeros((rows, D), F32).at[:B].set(c)
    tn = 1024 if N % 1024 == 0 else 512
    assert N % tn == 0
    out = pl.pallas_call(
        _ada_kernel,
        out_shape=jax.ShapeDtypeStruct((rows, N), F32),
        grid=(N // tn,),
        in_specs=[pl.BlockSpec((rows, D), lambda j: (0, 0)),
                  pl.BlockSpec((D, tn), lambda j: (0, j)),
                  pl.BlockSpec((1, tn), lambda j: (0, j))],
        out_specs=pl.BlockSpec((rows, tn), lambda j: (0, j)),
        compiler_params=_cparams(("arbitrary",)),
        name="ada_mod",
    )(cp, w_ada, b_ada.reshape(1, N))
    return out[:B]


def _head_norm(z, gain):
    ms = jnp.mean(z * z, axis=-1, keepdims=True)
    return z * lax.rsqrt(ms + EPS) * gain


def _qkv_kernel(x_ref, g1_ref, sc_ref, sh_ref, w_ref, qna_ref, kna_ref, qnb_ref, knb_ref,
                cos_ref, sin_ref, qa_ref, ka_ref, va_ref, qb_ref, kb_ref, vb_ref):
    x = x_ref[0]
    ms = jnp.mean(x * x, axis=-1, keepdims=True)
    h = (x * lax.rsqrt(ms + EPS) * g1_ref[...]) * (1.0 + sc_ref[0]) + sh_ref[0]
    z = jnp.dot(h.astype(BF16), w_ref[...], preferred_element_type=F32)

    cos = cos_ref[...]
    sin = sin_ref[...]
    lane = lax.broadcasted_iota(jnp.int32, cos.shape, 1)
    first_half = (lane % (ROPE_AXIS_DIM)) < (ROPE_AXIS_DIM // 2)
    scale = 1.0 / math.sqrt(HEAD_DIM)

    def rope(v):
        partner = jnp.where(first_half,
                            pltpu.roll(v, HEAD_DIM - ROPE_AXIS_DIM // 2, 1),
                            pltpu.roll(v, ROPE_AXIS_DIM // 2, 1))
        return v * cos + partner * sin

    def head(col):
        return z[:, col * HEAD_DIM:(col + 1) * HEAD_DIM]

    col = 0
    for hh in range(A_Q_HEADS):
        qa_ref[0, hh] = (rope(_head_norm(head(col), qna_ref[...])) * scale).astype(BF16)
        col += 1
    for hh in range(A_KV_HEADS):
        ka_ref[0, hh] = rope(_head_norm(head(col), kna_ref[...])).astype(BF16)
        col += 1
    for hh in range(A_KV_HEADS):
        va_ref[0, hh] = head(col).astype(BF16)
        col += 1
    for hh in range(B_Q_HEADS):
        qb_ref[0, hh] = (_head_norm(head(col), qnb_ref[...]) * scale).astype(BF16)
        col += 1
    for hh in range(B_KV_HEADS):
        kb_ref[0, hh] = _head_norm(head(col), knb_ref[...]).astype(BF16)
        col += 1
    for hh in range(B_KV_HEADS):
        vb_ref[0, hh] = head(col).astype(BF16)
        col += 1


def _rope_tables(S):
    rows = S // GRID_W
    row = jnp.repeat(jnp.arange(rows, dtype=F32), GRID_W)
    col = jnp.tile(jnp.arange(GRID_W, dtype=F32), rows)
    inv = 1.0 / (ROPE_THETA ** (jnp.arange(0, ROPE_AXIS_DIM, 2, dtype=F32) / ROPE_AXIS_DIM))
    ang_r = row[:, None] * inv[None, :]
    ang_c = col[:, None] * inv[None, :]
    cr, sr, cc, sc = jnp.cos(ang_r), jnp.sin(ang_r), jnp.cos(ang_c), jnp.sin(ang_c)
    cos = jnp.concatenate([cr, cr, cc, cc], axis=-1)
    sin = jnp.concatenate([-sr, sr, -sc, sc], axis=-1)
    return cos, sin


def _qkv_proj(x, g1, sc1, sh1, w_in_bf, qn_a, kn_a, qn_b, kn_b, tm):
    B, S, D = x.shape
    d_in = w_in_bf.shape[1]
    cos, sin = _rope_tables(S)
    vec = lambda v: v.reshape(1, -1).astype(F32)
    const = lambda shape: pl.BlockSpec(shape, lambda b, i: (0,) * len(shape))
    head_out = lambda nh: pl.BlockSpec((1, nh, tm, HEAD_DIM), lambda b, i: (b, 0, i, 0))
    head_shape = lambda nh: jax.ShapeDtypeStruct((B, nh, S, HEAD_DIM), BF16)
    return pl.pallas_call(
        _qkv_kernel,
        out_shape=(head_shape(A_Q_HEADS), head_shape(A_KV_HEADS), head_shape(A_KV_HEADS),
                   head_shape(B_Q_HEADS), head_shape(B_KV_HEADS), head_shape(B_KV_HEADS)),
        grid=(B, S // tm),
        in_specs=[pl.BlockSpec((1, tm, D), lambda b, i: (b, i, 0)),
                  const((1, D)),
                  pl.BlockSpec((1, 1, D), lambda b, i: (b, 0, 0)),
                  pl.BlockSpec((1, 1, D), lambda b, i: (b, 0, 0)),
                  pl.BlockSpec((D, d_in), lambda b, i: (0, 0), pipeline_mode=pl.Buffered(1)),
                  const((1, HEAD_DIM)), const((1, HEAD_DIM)), const((1, HEAD_DIM)), const((1, HEAD_DIM)),
                  pl.BlockSpec((tm, HEAD_DIM), lambda b, i: (i, 0)),
                  pl.BlockSpec((tm, HEAD_DIM), lambda b, i: (i, 0))],
        out_specs=(head_out(A_Q_HEADS), head_out(A_KV_HEADS), head_out(A_KV_HEADS),
                   head_out(B_Q_HEADS), head_out(B_KV_HEADS), head_out(B_KV_HEADS)),
        compiler_params=_cparams(("parallel", "parallel")),
        name="qkv_proj",
    )(x, vec(g1), sc1, sh1, w_in_bf, vec(qn_a), vec(kn_a), vec(qn_b), vec(kn_b), cos, sin)


def _attn_a_kernel(q_ref, k_ref, v_ref, o_ref, *, tk):
    G, tq, d = q_ref.shape[1:]
    S = k_ref.shape[2]
    q = q_ref[0].reshape(G * tq, d)

    def body(j, carry):
        m, l, acc = carry
        k0 = pl.multiple_of(j * tk, tk)
        k = k_ref[0, 0, pl.ds(k0, tk), :]
        v = v_ref[0, 0, pl.ds(k0, tk), :]
        s = lax.dot_general(q, k, (((1,), (1,)), ((), ())), preferred_element_type=F32)
        m_new = jnp.maximum(m, jnp.max(s, axis=-1, keepdims=True))
        alpha = jnp.exp(m - m_new)
        p = jnp.exp(s - m_new)
        l = alpha * l + jnp.sum(p, axis=-1, keepdims=True)
        acc = alpha * acc + jnp.dot(p.astype(BF16), v, preferred_element_type=F32)
        return m_new, l, acc

    m0 = jnp.full((G * tq, 1), -jnp.inf, F32)
    l0 = jnp.zeros((G * tq, 1), F32)
    a0 = jnp.zeros((G * tq, d), F32)
    _, l, acc = lax.fori_loop(0, S // tk, body, (m0, l0, a0))
    out = (acc / l).astype(o_ref.dtype)
    for g in range(G):
        o_ref[0, :, g * d:(g + 1) * d] = out[g * tq:(g + 1) * tq]


def _attn_a(q, k, v, tq, tk):
    B, Hq, S, d = q.shape
    Hkv = k.shape[1]
    G = Hq // Hkv
    return pl.pallas_call(
        functools.partial(_attn_a_kernel, tk=tk),
        out_shape=jax.ShapeDtypeStruct((B, S, Hq * d), BF16),
        grid=(B, Hkv, S // tq),
        in_specs=[pl.BlockSpec((1, G, tq, d), lambda b, h, i: (b, h, i, 0)),
                  pl.BlockSpec((1, 1, S, d), lambda b, h, i: (b, h, 0, 0)),
                  pl.BlockSpec((1, 1, S, d), lambda b, h, i: (b, h, 0, 0))],
        out_specs=pl.BlockSpec((1, tq, G * d), lambda b, h, i: (b, i, h)),
        compiler_params=_cparams(("parallel", "parallel", "arbitrary")),
        name="attn_global",
    )(q, k, v)


def _attn_b_kernel(rb_ref, q_ref, k_ref, v_ref, bucket_ref, sink_ref, o_ref, bias_ref, *, unroll):
    G, S, d = q_ref.shape[1:]
    kw = 3 * BLOCK
    h = pl.program_id(1)

    bucket = bucket_ref[...]
    for g in range(G):
        tab = jnp.full(bucket.shape, NEG_BIG, F32)
        for kb in range(N_BUCKETS):
            tab = jnp.where(bucket == kb, rb_ref[kb, h * G + g], tab)
        bias_ref[g] = tab

    sink = sink_ref[...]
    koff = lax.broadcasted_iota(jnp.int32, (1, 1, kw), 2) - BLOCK

    def block(n):
        r0 = pl.multiple_of(n * BLOCK, BLOCK)
        q = q_ref[0, :, pl.ds(r0, BLOCK), :].reshape(G * BLOCK, d)
        k = k_ref[0, 0, pl.ds(r0, kw), :]
        v = v_ref[0, 0, pl.ds(r0, kw), :]
        s = lax.dot_general(q, k, (((1,), (1,)), ((), ())), preferred_element_type=F32)
        s = s.reshape(G, BLOCK, kw) + bias_ref[...]
        kpos = koff + r0
        s = jnp.where((kpos >= 0) & (kpos < S), s, NEG_BIG)
        m = jnp.maximum(jnp.max(s, axis=-1, keepdims=True), sink)
        p = jnp.exp(s - m).reshape(G * BLOCK, kw).astype(BF16)
        ov = jnp.dot(p, v, preferred_element_type=F32).reshape(G, BLOCK, 2 * d)
        denom = ov[:, :, d:] + jnp.exp(sink - m)
        o = (ov[:, :, :d] / denom).astype(o_ref.dtype)
        for g in range(G):
            o_ref[0, pl.ds(r0, BLOCK), g * d:(g + 1) * d] = o[g]

    def body(i, carry):
        for u in range(unroll):
            block(i * unroll + u)
        return carry

    lax.fori_loop(0, S // (BLOCK * unroll), body, 0)


def _t5_bucket(rel):
    nb = N_BUCKETS // 2
    ret = jnp.where(rel > 0, nb, 0)
    n = jnp.abs(rel)
    max_exact = nb // 2
    nf = jnp.maximum(n, 1).astype(F32)
    large = max_exact + (jnp.log(nf / max_exact) / math.log(MAX_DISTANCE / max_exact)
                         * (nb - max_exact)).astype(jnp.int32)
    large = jnp.minimum(large, nb - 1)
    return ret + jnp.where(n < max_exact, n, large)


def _window_buckets():
    m = jnp.arange(3 * BLOCK)
    j = jnp.arange(BLOCK)
    rel = m[None, :] - BLOCK - j[:, None]
    return jnp.where(jnp.abs(rel) <= WINDOW, _t5_bucket(rel), -1).astype(jnp.int32)


def _attn_b(q, k, v, rel_bias, sink):
    B, Hq, S, d = q.shape
    Hkv = k.shape[1]
    G = Hq // Hkv
    nblk = S // BLOCK
    unroll = 4 if nblk % 4 == 0 else 1
    pad = ((0, 0), (0, 0), (BLOCK, BLOCK), (0, 0))
    kp = jnp.pad(k, pad)
    vp = jnp.pad(jnp.concatenate([v, jnp.ones_like(v)], axis=-1), pad)
    return pl.pallas_call(
        functools.partial(_attn_b_kernel, unroll=unroll),
        out_shape=jax.ShapeDtypeStruct((B, S, Hq * d), BF16),
        grid=(B, Hkv),
        in_specs=[pl.BlockSpec(memory_space=pltpu.SMEM),
                  pl.BlockSpec((1, G, S, d), lambda b, h: (b, h, 0, 0)),
                  pl.BlockSpec((1, 1, S + 2 * BLOCK, d), lambda b, h: (b, h, 0, 0)),
                  pl.BlockSpec((1, 1, S + 2 * BLOCK, 2 * d), lambda b, h: (b, h, 0, 0)),
                  pl.BlockSpec((BLOCK, 3 * BLOCK), lambda b, h: (0, 0)),
                  pl.BlockSpec((G, 1, 1), lambda b, h: (h, 0, 0))],
        out_specs=pl.BlockSpec((1, S, G * d), lambda b, h: (b, 0, h)),
        scratch_shapes=[pltpu.VMEM((G, BLOCK, 3 * BLOCK), F32)],
        compiler_params=_cparams(("parallel", "parallel")),
        name="attn_window",
    )(rel_bias.astype(F32), q, kp, vp, _window_buckets(), sink.astype(F32).reshape(Hq, 1, 1))


def _outproj_kernel(oa_ref, ob_ref, x_ref, w_ref, gt_ref, g2_ref, sc_ref, sh_ref, wr_ref,
                    x1_ref, h2_ref, aff_ref):
    half = oa_ref.shape[2]
    mix = jnp.dot(oa_ref[0], w_ref[:half, :], preferred_element_type=F32)
    mix = mix + jnp.dot(ob_ref[0], w_ref[half:, :], preferred_element_type=F32)
    x1 = x_ref[0] + gt_ref[0] * mix
    x1_ref[0] = x1
    ms = jnp.mean(x1 * x1, axis=-1, keepdims=True)
    h2 = (x1 * lax.rsqrt(ms + EPS) * g2_ref[...]) * (1.0 + sc_ref[0]) + sh_ref[0]
    h2_ref[0] = h2
    logits = jnp.dot(h2.astype(BF16), wr_ref[...], preferred_element_type=F32)
    e = jnp.exp(logits - jnp.max(logits, axis=-1, keepdims=True))
    aff_ref[0] = e / jnp.sum(e, axis=-1, keepdims=True)


def _outproj(oa, ob, x, w_out_bf, gt1, g2, sc2, sh2, wr_bf, tm):
    B, S, D = x.shape
    half = oa.shape[2]
    E = wr_bf.shape[1]
    const = lambda shape: pl.BlockSpec(shape, lambda b, i: (0,) * len(shape))
    per_b = pl.BlockSpec((1, 1, D), lambda b, i: (b, 0, 0))
    return pl.pallas_call(
        _outproj_kernel,
        out_shape=(jax.ShapeDtypeStruct((B, S, D), F32),
                   jax.ShapeDtypeStruct((B, S, D), F32),
                   jax.ShapeDtypeStruct((B, S, E), F32)),
        grid=(B, S // tm),
        in_specs=[pl.BlockSpec((1, tm, half), lambda b, i: (b, i, 0)),
                  pl.BlockSpec((1, tm, half), lambda b, i: (b, i, 0)),
                  pl.BlockSpec((1, tm, D), lambda b, i: (b, i, 0)),
                  pl.BlockSpec((2 * half, D), lambda b, i: (0, 0), pipeline_mode=pl.Buffered(1)),
                  per_b, const((1, D)), per_b, per_b,
                  const((D, E))],
        out_specs=(pl.BlockSpec((1, tm, D), lambda b, i: (b, i, 0)),
                   pl.BlockSpec((1, tm, D), lambda b, i: (b, i, 0)),
                   pl.BlockSpec((1, tm, E), lambda b, i: (b, i, 0))),
        compiler_params=_cparams(("parallel", "parallel")),
        name="outproj_router",
    )(oa, ob, x, w_out_bf, gt1, g2.reshape(1, D).astype(F32), sc2, sh2, wr_bf)


def _group_allreduce(v, op):
    shift = N_EXPERTS
    while shift < LANES:
        v = op(v, pltpu.roll(v, shift, 1))
        shift *= 2
    return v


def _select_kernel(aff_ref, idx_ref, rot_ref, *, cap, n_bisect):
    B, R, L = aff_ref.shape
    a = aff_ref[...]
    capf = jnp.float32(cap)

    def count(mask):
        return _group_allreduce(jnp.sum(jnp.where(mask, 1.0, 0.0), axis=1), jnp.add)

    def bis(_, carry):
        lo, hi = carry
        mid = 0.5 * (lo + hi)
        ge = count(a > mid[:, None, :]) >= capf
        return jnp.where(ge, mid, lo), jnp.where(ge, hi, mid)

    lo0 = jnp.full((B, L), -1.0, F32)
    hi0 = jnp.full((B, L), 1.0, F32)
    lo, _ = lax.fori_loop(0, n_bisect, bis, (lo0, hi0))
    tau = jnp.min(jnp.where(a > lo[:, None, :], a, jnp.inf), axis=1)
    tau = _group_allreduce(tau, jnp.minimum)[:, None, :]

    lane_r = lax.broadcasted_iota(jnp.int32, (L, L), 0)
    lane_c = lax.broadcasted_iota(jnp.int32, (L, L), 1)
    same_e = (lane_r % N_EXPERTS) == (lane_c % N_EXPERTS)
    g_incl = jnp.where(same_e & (lane_r // N_EXPERTS <= lane_c // N_EXPERTS), 1.0, 0.0).astype(BF16)
    g_full = jnp.where(same_e, 1.0, 0.0).astype(BF16)
    row_r = lax.broadcasted_iota(jnp.int32, (R, R), 0)
    row_c = lax.broadcasted_iota(jnp.int32, (R, R), 1)
    l_strict = jnp.where(row_c < row_r, 1.0, 0.0).astype(BF16)

    def prefix_incl(mask):
        ones = jnp.where(mask, 1.0, 0.0)
        outs = []
        for b in range(B):
            mb = ones[b].astype(BF16)
            within = jnp.dot(mb, g_incl, preferred_element_type=F32)
            tot = jnp.dot(mb, g_full, preferred_element_type=F32)
            before = jnp.dot(l_strict, tot.astype(BF16), preferred_element_type=F32)
            outs.append((within + before)[None])
        return jnp.concatenate(outs, axis=0)

    gt = a > tau
    eq = a == tau
    need = (capf - count(gt))[:, None, :]
    eq_before = prefix_incl(eq) - jnp.where(eq, 1.0, 0.0)
    sel = gt | (eq & (eq_before < need))
    rank = prefix_incl(sel)

    rank2 = rank.reshape(B * R, L)
    for rho in range(TOK_PER_ROW):
        rot_ref[rho] = rank2 if rho == 0 else pltpu.roll(rank2, rho * N_EXPERTS, 1)
    slot_in_row = (lax.broadcasted_iota(jnp.int32, (1, L), 1) // N_EXPERTS).astype(F32)

    def slots(i, carry):
        cvec = slot_in_row + jnp.asarray(i * TOK_PER_ROW, F32)
        tot = jnp.zeros((B, L), F32)
        for rho in range(TOK_PER_ROW):
            le = jnp.where(rot_ref[rho] <= cvec, 1.0, 0.0)
            tot = tot + jnp.sum(le.reshape(B, R, L), axis=1)
        idx_ref[:, pl.ds(i, 1), :] = tot.astype(jnp.int32)[:, None, :]
        return carry

    lax.fori_loop(0, cap // TOK_PER_ROW, slots, 0)


def _select(aff, cap):
    B, S, E = aff.shape
    R = S // TOK_PER_ROW
    packed = aff.reshape(B, R, LANES)
    idx = pl.pallas_call(
        functools.partial(_select_kernel, cap=cap, n_bisect=160),
        out_shape=jax.ShapeDtypeStruct((B, cap // TOK_PER_ROW, LANES), jnp.int32),
        grid=(1,),
        in_specs=[pl.BlockSpec((B, R, LANES), lambda i: (0, 0, 0))],
        out_specs=pl.BlockSpec((B, cap // TOK_PER_ROW, LANES), lambda i: (0, 0, 0)),
        scratch_shapes=[pltpu.VMEM((TOK_PER_ROW, B * R, LANES), F32)],
        compiler_params=_cparams(("arbitrary",)),
        name="expert_select",
    )(packed)
    return idx.reshape(B, cap, E).transpose(0, 2, 1)


def _expert_kernel(idx_ref, h2_hbm, acc_hbm, wg_ref, wu_ref, wd_ref, wr_ref, gt_ref,
                   out_hbm, xb_ref, y_ref, g_ref, sem, *, cap, tn):
    del acc_hbm
    e = pl.program_id(0)
    hf = pl.program_id(1)
    f = pl.program_id(2)
    nf = pl.num_programs(2)
    D = y_ref.shape[2]
    M = y_ref.shape[0] * SUBLANES
    base = (e * pl.num_programs(1) + hf) * M

    def start_rows(hbm, to_hbm):
        def body(i, carry):
            for u in range(SUBLANES):
                row = idx_ref[base + i * SUBLANES + u]
                if to_hbm:
                    pltpu.make_async_copy(y_ref.at[i, pl.ds(u, 1)], hbm.at[pl.ds(row, 1)], sem).start()
                else:
                    pltpu.make_async_copy(hbm.at[pl.ds(row, 1)], y_ref.at[i, pl.ds(u, 1)], sem).start()
            return carry
        lax.fori_loop(0, M // SUBLANES, body, 0)

    def wait_rows():
        pltpu.make_async_copy(y_ref, y_ref, sem).wait()

    @pl.when(f == 0)
    def _():
        start_rows(h2_hbm, False)
        wait_rows()
        xb = y_ref[...].reshape(M, D).astype(BF16)
        xb_ref[...] = xb
        start_rows(out_hbm, False)
        logits = jnp.dot(xb, wr_ref[...], preferred_element_type=F32)
        ex = jnp.exp(logits - jnp.max(logits, axis=-1, keepdims=True))
        aff = ex / jnp.sum(ex, axis=-1, keepdims=True)
        lane = lax.broadcasted_iota(jnp.int32, aff.shape, 1)
        g_ref[...] = jnp.sum(jnp.where(lane == e, aff, 0.0), axis=-1, keepdims=True)

    xb = xb_ref[...]
    a = jnp.dot(xb, wg_ref[0].astype(BF16), preferred_element_type=F32)
    u = jnp.dot(xb, wu_ref[0].astype(BF16), preferred_element_type=F32)
    hmid = ((a * jax.nn.sigmoid(a)) * u).astype(BF16)

    @pl.when(f == 0)
    def _():
        wait_rows()

    g = g_ref[...]
    nb = gt_ref.shape[0]
    ct = cap // SUBLANES
    for n0 in range(0, D, tn):
        yc = jnp.dot(hmid, wd_ref[0, :, n0:n0 + tn].astype(BF16), preferred_element_type=F32)
        yc = yc * g
        for b in range(nb):
            contrib = gt_ref[b, :, n0:n0 + tn] * yc[b * cap:(b + 1) * cap]
            y_ref[b * ct:(b + 1) * ct, :, n0:n0 + tn] += contrib.reshape(ct, SUBLANES, tn)

    @pl.when(f == nf - 1)
    def _():
        start_rows(out_hbm, True)
        wait_rows()


def _experts(idx_rows, h2, x1, w_gate, w_up, w_down, wr_bf, gt2, cap, n_half, tf, tn):
    N, D = h2.shape
    E, _, F = w_gate.shape
    B = gt2.shape[0]
    bh = B // n_half
    M = bh * cap
    grid_spec = pltpu.PrefetchScalarGridSpec(
        num_scalar_prefetch=1,
        grid=(E, n_half, F // tf),
        in_specs=[pl.BlockSpec(memory_space=pl.ANY),
                  pl.BlockSpec(memory_space=pl.ANY),
                  pl.BlockSpec((1, D, tf), lambda e, h, f, idx: (e, 0, f)),
                  pl.BlockSpec((1, D, tf), lambda e, h, f, idx: (e, 0, f)),
                  pl.BlockSpec((1, tf, D), lambda e, h, f, idx: (e, f, 0)),
                  pl.BlockSpec((D, E), lambda e, h, f, idx: (0, 0)),
                  pl.BlockSpec((bh, 1, D), lambda e, h, f, idx: (h, 0, 0))],
        out_specs=pl.BlockSpec(memory_space=pl.ANY),
        scratch_shapes=[pltpu.VMEM((M, D), BF16),
                        pltpu.VMEM((M // SUBLANES, SUBLANES, D), F32),
                        pltpu.VMEM((M, 1), F32),
                        pltpu.SemaphoreType.DMA],
    )
    return pl.pallas_call(
        functools.partial(_expert_kernel, cap=cap, tn=tn),
        out_shape=jax.ShapeDtypeStruct((N, D), F32),
        grid_spec=grid_spec,
        input_output_aliases={2: 0},
        compiler_params=_cparams(("arbitrary", "arbitrary", "arbitrary")),
        name="moe_experts",
    )(idx_rows, h2, x1, w_gate, w_up, w_down, wr_bf, gt2)


def kernel(x, c, w_ada, b_ada, g_norm1, w_in, qn_a, kn_a, qn_b, kn_b, sink_b, rel_bias,
           w_out, g_norm2, w_router, w_gate, w_up, w_down):
    B, S, D = x.shape
    E = w_router.shape[1]
    cap = CAPACITY_FACTOR * S // E

    mod = _ada_mod(c, w_ada, b_ada).reshape(B, 6, 1, D)
    sh1, sc1, gt1, sh2, sc2, gt2 = (mod[:, i] for i in range(6))

    tm = min(512, S)
    qa, ka, va, qb, kb, vb = _qkv_proj(x, g_norm1, sc1, sh1, w_in.astype(BF16),
                                       qn_a, kn_a, qn_b, kn_b, tm)
    oa = _attn_a(qa, ka, va, tq=min(256, S), tk=min(512, S))
    ob = _attn_b(qb, kb, vb, rel_bias, sink_b)

    x1, h2, aff = _outproj(oa, ob, x, w_out.astype(BF16), gt1, g_norm2, sc2, sh2,
                           w_router.astype(BF16), tm)

    idx = _select(aff, cap)
    rows = idx + (jnp.arange(B, dtype=jnp.int32) * S)[:, None, None]
    rows = rows.transpose(1, 0, 2).reshape(-1)

    out = _experts(rows, h2.reshape(B * S, D), x1.reshape(B * S, D), w_gate, w_up, w_down,
                   w_router.astype(BF16), gt2, cap, n_half=2 if B % 2 == 0 else 1,
                   tf=256, tn=min(512, D))
    return out.reshape(B, S, D)
```

```python
import functools
import math

import jax
import jax.numpy as jnp
from jax import lax
from jax.experimental import pallas as pl
from jax.experimental.pallas import tpu as pltpu

HEAD_DIM = 128
A_Q_HEADS = 8
A_KV_HEADS = 2
B_Q_HEADS = 8
B_KV_HEADS = 2
GQA_GROUP = 4
BLOCK = 128
WINDOW = 128
GRID_W = 64
ROPE_THETA = 10000.0
ROPE_AXIS_DIM = HEAD_DIM // 2
N_BUCKETS = 32
MAX_DISTANCE = 128
N_EXPERTS = 16
CAPACITY_FACTOR = 2
EPS = 1e-6
NEG_BIG = -1e30
LOG2E = math.log2(math.e)

LANES = 128
TOK_PER_ROW = LANES // N_EXPERTS
VMEM_LIMIT = 56 * 1024 * 1024
SUBLANES = 8

F32 = jnp.float32
BF16 = jnp.bfloat16


def _cparams(sem):
    return pltpu.CompilerParams(dimension_semantics=sem, vmem_limit_bytes=VMEM_LIMIT)


def _ada_kernel(c_ref, w_ref, b_ref, o_ref):
    c = c_ref[...]
    s = c * jax.nn.sigmoid(c)
    o_ref[...] = jnp.dot(s.astype(BF16), w_ref[...].astype(BF16),
                         preferred_element_type=F32) + b_ref[...]


def _ada_mod(c, w_ada, b_ada):
    B, D = c.shape
    N = w_ada.shape[1]
    rows = 8
    cp = jnp.zeros((rows, D), F32).at[:B].set(c)
    tn = 1024 if N % 1024 == 0 else 512
    assert N % tn == 0
    out = pl.pallas_call(
        _ada_kernel,
        out_shape=jax.ShapeDtypeStruct((rows, N), F32),
        grid=(N // tn,),
        in_specs=[pl.BlockSpec((rows, D), lambda j: (0, 0)),
                  pl.BlockSpec((D, tn), lambda j: (0, j)),
                  pl.BlockSpec((1, tn), lambda j: (0, j))],
        out_specs=pl.BlockSpec((rows, tn), lambda j: (0, j)),
        compiler_params=_cparams(("arbitrary",)),
        name="ada_mod",
    )(cp, w_ada, b_ada.reshape(1, N))
    return out[:B]


def _head_norm(z, gain):
    ms = jnp.mean(z * z, axis=-1, keepdims=True)
    return z * lax.rsqrt(ms + EPS) * gain


def _qkv_kernel(x_ref, g1_ref, sc_ref, sh_ref, w_ref, qna_ref, kna_ref, qnb_ref, knb_ref,
                cos_ref, sin_ref, qa_ref, ka_ref, va_ref, qb_ref, kb_ref, vb_ref):
    x = x_ref[0]
    ms = jnp.mean(x * x, axis=-1, keepdims=True)
    h = (x * lax.rsqrt(ms + EPS) * g1_ref[...]) * (1.0 + sc_ref[0]) + sh_ref[0]
    z = jnp.dot(h.astype(BF16), w_ref[...], preferred_element_type=F32)

    cos = cos_ref[...]
    sin = sin_ref[...]
    lane = lax.broadcasted_iota(jnp.int32, cos.shape, 1)
    first_half = (lane % (ROPE_AXIS_DIM)) < (ROPE_AXIS_DIM // 2)
    scale = 1.0 / math.sqrt(HEAD_DIM)

    def rope(v):
        partner = jnp.where(first_half,
                            pltpu.roll(v, HEAD_DIM - ROPE_AXIS_DIM // 2, 1),
                            pltpu.roll(v, ROPE_AXIS_DIM // 2, 1))
        return v * cos + partner * sin

    def head(col):
        return z[:, col * HEAD_DIM:(col + 1) * HEAD_DIM]

    col = 0
    for hh in range(A_Q_HEADS):
        qa_ref[0, hh] = (rope(_head_norm(head(col), qna_ref[...])) * (scale * LOG2E)).astype(BF16)
        col += 1
    for hh in range(A_KV_HEADS):
        ka_ref[0, hh] = rope(_head_norm(head(col), kna_ref[...])).astype(BF16)
        col += 1
    for hh in range(A_KV_HEADS):
        va_ref[0, hh] = head(col).astype(BF16)
        col += 1
    for hh in range(B_Q_HEADS):
        qb_ref[0, hh] = (_head_norm(head(col), qnb_ref[...]) * scale).astype(BF16)
        col += 1
    for hh in range(B_KV_HEADS):
        kb_ref[0, hh] = _head_norm(head(col), knb_ref[...]).astype(BF16)
        col += 1
    for hh in range(B_KV_HEADS):
        vb_ref[0, hh] = head(col).astype(BF16)
        col += 1


def _rope_tables(S):
    rows = S // GRID_W
    row = jnp.repeat(jnp.arange(rows, dtype=F32), GRID_W)
    col = jnp.tile(jnp.arange(GRID_W, dtype=F32), rows)
    inv = 1.0 / (ROPE_THETA ** (jnp.arange(0, ROPE_AXIS_DIM, 2, dtype=F32) / ROPE_AXIS_DIM))
    ang_r = row[:, None] * inv[None, :]
    ang_c = col[:, None] * inv[None, :]
    cr, sr, cc, sc = jnp.cos(ang_r), jnp.sin(ang_r), jnp.cos(ang_c), jnp.sin(ang_c)
    cos = jnp.concatenate([cr, cr, cc, cc], axis=-1)
    sin = jnp.concatenate([-sr, sr, -sc, sc], axis=-1)
    return cos, sin


def _qkv_proj(x, g1, sc1, sh1, w_in_bf, qn_a, kn_a, qn_b, kn_b, tm):
    B, S, D = x.shape
    d_in = w_in_bf.shape[1]
    cos, sin = _rope_tables(S)
    vec = lambda v: v.reshape(1, -1).astype(F32)
    const = lambda shape: pl.BlockSpec(shape, lambda b, i: (0,) * len(shape))
    head_out = lambda nh: pl.BlockSpec((1, nh, tm, HEAD_DIM), lambda b, i: (b, 0, i, 0))
    head_shape = lambda nh: jax.ShapeDtypeStruct((B, nh, S, HEAD_DIM), BF16)
    return pl.pallas_call(
        _qkv_kernel,
        out_shape=(head_shape(A_Q_HEADS), head_shape(A_KV_HEADS), head_shape(A_KV_HEADS),
                   head_shape(B_Q_HEADS), head_shape(B_KV_HEADS), head_shape(B_KV_HEADS)),
        grid=(B, S // tm),
        in_specs=[pl.BlockSpec((1, tm, D), lambda b, i: (b, i, 0)),
                  const((1, D)),
                  pl.BlockSpec((1, 1, D), lambda b, i: (b, 0, 0)),
                  pl.BlockSpec((1, 1, D), lambda b, i: (b, 0, 0)),
                  pl.BlockSpec((D, d_in), lambda b, i: (0, 0), pipeline_mode=pl.Buffered(1)),
                  const((1, HEAD_DIM)), const((1, HEAD_DIM)), const((1, HEAD_DIM)), const((1, HEAD_DIM)),
                  pl.BlockSpec((tm, HEAD_DIM), lambda b, i: (i, 0)),
                  pl.BlockSpec((tm, HEAD_DIM), lambda b, i: (i, 0))],
        out_specs=(head_out(A_Q_HEADS), head_out(A_KV_HEADS), head_out(A_KV_HEADS),
                   head_out(B_Q_HEADS), head_out(B_KV_HEADS), head_out(B_KV_HEADS)),
        compiler_params=_cparams(("parallel", "parallel")),
        name="qkv_proj",
    )(x, vec(g1), sc1, sh1, w_in_bf, vec(qn_a), vec(kn_a), vec(qn_b), vec(kn_b), cos, sin)


def _attn_a_kernel(q_ref, k_ref, vt_ref, o_ref, s_ref, acc_ref, m_ref, l_ref, *, tk):
    G, tq, d = q_ref.shape[1:]
    S = k_ref.shape[2]
    R = G * tq
    nk = S // tk
    q = q_ref[0].reshape(R, d)

    def scores(j):
        k0 = pl.multiple_of(j * tk, tk)
        k = k_ref[0, 0, pl.ds(k0, tk), :]
        return lax.dot_general(k, q, (((1,), (1,)), ((), ())), preferred_element_type=F32)

    def update(slot, j):
        k0 = pl.multiple_of(j * tk, tk)
        s = s_ref[slot]
        m = m_ref[...]
        m_new = jnp.maximum(m, jnp.max(s, axis=0, keepdims=True))
        alpha = jnp.exp2(m - m_new)
        p = jnp.exp2(s - m_new)
        l_ref[...] = alpha * l_ref[...] + jnp.sum(p, axis=0, keepdims=True)
        vt = vt_ref[0, 0, :, pl.ds(k0, tk)]
        acc_ref[...] = alpha * acc_ref[...] + jnp.dot(vt, p.astype(BF16),
                                                      preferred_element_type=F32)
        m_ref[...] = m_new

    m_ref[...] = jnp.full((1, R), -jnp.inf, F32)
    l_ref[...] = jnp.zeros((1, R), F32)
    acc_ref[...] = jnp.zeros((d, R), F32)
    s_ref[0] = scores(0)

    def body(jj, carry):
        j = 2 * jj
        s_ref[1] = scores(j + 1)
        update(0, j)
        s_ref[0] = scores(jnp.minimum(j + 2, nk - 1))
        update(1, j + 1)
        return carry

    lax.fori_loop(0, nk // 2, body, 0)
    out = (acc_ref[...] / l_ref[...]).T.astype(o_ref.dtype)
    for g in range(G):
        o_ref[0, :, g * d:(g + 1) * d] = out[g * tq:(g + 1) * tq]


def _attn_a(q, k, v, tq, tk):
    B, Hq, S, d = q.shape
    Hkv = k.shape[1]
    G = Hq // Hkv
    vt = v.transpose(0, 1, 3, 2)
    R = G * tq
    assert (S // tk) % 2 == 0
    return pl.pallas_call(
        functools.partial(_attn_a_kernel, tk=tk),
        out_shape=jax.ShapeDtypeStruct((B, S, Hq * d), BF16),
        grid=(B, Hkv, S // tq),
        in_specs=[pl.BlockSpec((1, G, tq, d), lambda b, h, i: (b, h, i, 0)),
                  pl.BlockSpec((1, 1, S, d), lambda b, h, i: (b, h, 0, 0)),
                  pl.BlockSpec((1, 1, d, S), lambda b, h, i: (b, h, 0, 0))],
        out_specs=pl.BlockSpec((1, tq, G * d), lambda b, h, i: (b, i, h)),
        scratch_shapes=[pltpu.VMEM((2, tk, R), F32), pltpu.VMEM((d, R), F32),
                        pltpu.VMEM((1, R), F32), pltpu.VMEM((1, R), F32)],
        compiler_params=_cparams(("parallel", "parallel", "arbitrary")),
        name="attn_global",
    )(q, k, vt)


def _attn_b_kernel(rb_ref, q_ref, k_ref, v_ref, bucket_ref, sink_ref, o_ref, bias_ref, *, unroll):
    G, S, d = q_ref.shape[1:]
    kw = 3 * BLOCK
    h = pl.program_id(1)

    bucket = bucket_ref[...]
    for g in range(G):
        tab = jnp.full(bucket.shape, NEG_BIG, F32)
        for kb in range(N_BUCKETS):
            tab = jnp.where(bucket == kb, rb_ref[kb, h * G + g], tab)
        bias_ref[g] = tab

    sink = sink_ref[...]
    koff = lax.broadcasted_iota(jnp.int32, (1, 1, kw), 2) - BLOCK

    def block(n):
        r0 = pl.multiple_of(n * BLOCK, BLOCK)
        q = q_ref[0, :, pl.ds(r0, BLOCK), :].reshape(G * BLOCK, d)
        k = k_ref[0, 0, pl.ds(r0, kw), :]
        v = v_ref[0, 0, pl.ds(r0, kw), :]
        s = lax.dot_general(q, k, (((1,), (1,)), ((), ())), preferred_element_type=F32)
        s = s.reshape(G, BLOCK, kw) + bias_ref[...]
        kpos = koff + r0
        s = jnp.where((kpos >= 0) & (kpos < S), s, NEG_BIG)
        m = jnp.maximum(jnp.max(s, axis=-1, keepdims=True), sink)
        p = jnp.exp(s - m).reshape(G * BLOCK, kw).astype(BF16)
        ov = jnp.dot(p, v, preferred_element_type=F32).reshape(G, BLOCK, 2 * d)
        denom = ov[:, :, d:] + jnp.exp(sink - m)
        o = (ov[:, :, :d] / denom).astype(o_ref.dtype)
        for g in range(G):
            o_ref[0, pl.ds(r0, BLOCK), g * d:(g + 1) * d] = o[g]

    def body(i, carry):
        for u in range(unroll):
            block(i * unroll + u)
        return carry

    lax.fori_loop(0, S // (BLOCK * unroll), body, 0)


def _t5_bucket(rel):
    nb = N_BUCKETS // 2
    ret = jnp.where(rel > 0, nb, 0)
    n = jnp.abs(rel)
    max_exact = nb // 2
    nf = jnp.maximum(n, 1).astype(F32)
    large = max_exact + (jnp.log(nf / max_exact) / math.log(MAX_DISTANCE / max_exact)
                         * (nb - max_exact)).astype(jnp.int32)
    large = jnp.minimum(large, nb - 1)
    return ret + jnp.where(n < max_exact, n, large)


def _window_buckets():
    m = jnp.arange(3 * BLOCK)
    j = jnp.arange(BLOCK)
    rel = m[None, :] - BLOCK - j[:, None]
    return jnp.where(jnp.abs(rel) <= WINDOW, _t5_bucket(rel), -1).astype(jnp.int32)


def _attn_b(q, k, v, rel_bias, sink):
    B, Hq, S, d = q.shape
    Hkv = k.shape[1]
    G = Hq // Hkv
    nblk = S // BLOCK
    unroll = 4 if nblk % 4 == 0 else 1
    pad = ((0, 0), (0, 0), (BLOCK, BLOCK), (0, 0))
    kp = jnp.pad(k, pad)
    vp = jnp.pad(jnp.concatenate([v, jnp.ones_like(v)], axis=-1), pad)
    return pl.pallas_call(
        functools.partial(_attn_b_kernel, unroll=unroll),
        out_shape=jax.ShapeDtypeStruct((B, S, Hq * d), BF16),
        grid=(B, Hkv),
        in_specs=[pl.BlockSpec(memory_space=pltpu.SMEM),
                  pl.BlockSpec((1, G, S, d), lambda b, h: (b, h, 0, 0)),
                  pl.BlockSpec((1, 1, S + 2 * BLOCK, d), lambda b, h: (b, h, 0, 0)),
                  pl.BlockSpec((1, 1, S + 2 * BLOCK, 2 * d), lambda b, h: (b, h, 0, 0)),
                  pl.BlockSpec((BLOCK, 3 * BLOCK), lambda b, h: (0, 0)),
                  pl.BlockSpec((G, 1, 1), lambda b, h: (h, 0, 0))],
        out_specs=pl.BlockSpec((1, S, G * d), lambda b, h: (b, 0, h)),
        scratch_shapes=[pltpu.VMEM((G, BLOCK, 3 * BLOCK), F32)],
        compiler_params=_cparams(("parallel", "parallel")),
        name="attn_window",
    )(rel_bias.astype(F32), q, kp, vp, _window_buckets(), sink.astype(F32).reshape(Hq, 1, 1))


def _outproj_kernel(oa_ref, ob_ref, x_ref, w_ref, gt_ref, g2_ref, sc_ref, sh_ref, wr_ref,
                    x1_ref, h2_ref, aff_ref):
    half = oa_ref.shape[2]
    mix = jnp.dot(oa_ref[0], w_ref[:half, :], preferred_element_type=F32)
    mix = mix + jnp.dot(ob_ref[0], w_ref[half:, :], preferred_element_type=F32)
    x1 = x_ref[0] + gt_ref[0] * mix
    x1_ref[0] = x1
    ms = jnp.mean(x1 * x1, axis=-1, keepdims=True)
    h2 = (x1 * lax.rsqrt(ms + EPS) * g2_ref[...]) * (1.0 + sc_ref[0]) + sh_ref[0]
    h2_ref[0] = h2
    logits = jnp.dot(h2.astype(BF16), wr_ref[...], preferred_element_type=F32)
    e = jnp.exp(logits - jnp.max(logits, axis=-1, keepdims=True))
    aff_ref[0] = e / jnp.sum(e, axis=-1, keepdims=True)


def _outproj(oa, ob, x, w_out_bf, gt1, g2, sc2, sh2, wr_bf, tm):
    B, S, D = x.shape
    half = oa.shape[2]
    E = wr_bf.shape[1]
    const = lambda shape: pl.BlockSpec(shape, lambda b, i: (0,) * len(shape))
    per_b = pl.BlockSpec((1, 1, D), lambda b, i: (b, 0, 0))
    return pl.pallas_call(
        _outproj_kernel,
        out_shape=(jax.ShapeDtypeStruct((B, S, D), F32),
                   jax.ShapeDtypeStruct((B, S, D), F32),
                   jax.ShapeDtypeStruct((B, S, E), F32)),
        grid=(B, S // tm),
        in_specs=[pl.BlockSpec((1, tm, half), lambda b, i: (b, i, 0)),
                  pl.BlockSpec((1, tm, half), lambda b, i: (b, i, 0)),
                  pl.BlockSpec((1, tm, D), lambda b, i: (b, i, 0)),
                  pl.BlockSpec((2 * half, D), lambda b, i: (0, 0), pipeline_mode=pl.Buffered(1)),
                  per_b, const((1, D)), per_b, per_b,
                  const((D, E))],
        out_specs=(pl.BlockSpec((1, tm, D), lambda b, i: (b, i, 0)),
                   pl.BlockSpec((1, tm, D), lambda b, i: (b, i, 0)),
                   pl.BlockSpec((1, tm, E), lambda b, i: (b, i, 0))),
        compiler_params=_cparams(("parallel", "parallel")),
        name="outproj_router",
    )(oa, ob, x, w_out_bf, gt1, g2.reshape(1, D).astype(F32), sc2, sh2, wr_bf)


def _group_allreduce(v, op):
    shift = N_EXPERTS
    while shift < LANES:
        v = op(v, pltpu.roll(v, shift, 1))
        shift *= 2
    return v


def _select_kernel(aff_ref, idx_ref, rot_ref, *, cap, n_bisect):
    B, R, L = aff_ref.shape
    a = aff_ref[...]
    capf = jnp.float32(cap)

    def count(mask):
        return _group_allreduce(jnp.sum(jnp.where(mask, 1.0, 0.0), axis=1), jnp.add)

    def bis(_, carry):
        lo, hi = carry
        mid = 0.5 * (lo + hi)
        ge = count(a > mid[:, None, :]) >= capf
        return jnp.where(ge, mid, lo), jnp.where(ge, hi, mid)

    lo0 = jnp.full((B, L), -1.0, F32)
    hi0 = jnp.full((B, L), 1.0, F32)
    lo, _ = lax.fori_loop(0, n_bisect, bis, (lo0, hi0))
    tau = jnp.min(jnp.where(a > lo[:, None, :], a, jnp.inf), axis=1)
    tau = _group_allreduce(tau, jnp.minimum)[:, None, :]

    lane_r = lax.broadcasted_iota(jnp.int32, (L, L), 0)
    lane_c = lax.broadcasted_iota(jnp.int32, (L, L), 1)
    same_e = (lane_r % N_EXPERTS) == (lane_c % N_EXPERTS)
    g_incl = jnp.where(same_e & (lane_r // N_EXPERTS <= lane_c // N_EXPERTS), 1.0, 0.0).astype(BF16)
    g_full = jnp.where(same_e, 1.0, 0.0).astype(BF16)
    row_r = lax.broadcasted_iota(jnp.int32, (R, R), 0)
    row_c = lax.broadcasted_iota(jnp.int32, (R, R), 1)
    l_strict = jnp.where(row_c < row_r, 1.0, 0.0).astype(BF16)

    def prefix_incl(mask):
        ones = jnp.where(mask, 1.0, 0.0)
        outs = []
        for b in range(B):
            mb = ones[b].astype(BF16)
            within = jnp.dot(mb, g_incl, preferred_element_type=F32)
            tot = jnp.dot(mb, g_full, preferred_element_type=F32)
            before = jnp.dot(l_strict, tot.astype(BF16), preferred_element_type=F32)
            outs.append((within + before)[None])
        return jnp.concatenate(outs, axis=0)

    gt = a > tau
    eq = a == tau
    need = (capf - count(gt))[:, None, :]
    eq_before = prefix_incl(eq) - jnp.where(eq, 1.0, 0.0)
    sel = gt | (eq & (eq_before < need))
    rank = prefix_incl(sel)

    rank2 = rank.reshape(B * R, L)
    for rho in range(TOK_PER_ROW):
        rot_ref[rho] = rank2 if rho == 0 else pltpu.roll(rank2, rho * N_EXPERTS, 1)
    slot_in_row = (lax.broadcasted_iota(jnp.int32, (1, L), 1) // N_EXPERTS).astype(F32)

    def slots(i, carry):
        cvec = slot_in_row + jnp.asarray(i * TOK_PER_ROW, F32)
        tot = jnp.zeros((B, L), F32)
        for rho in range(TOK_PER_ROW):
            le = jnp.where(rot_ref[rho] <= cvec, 1.0, 0.0)
            tot = tot + jnp.sum(le.reshape(B, R, L), axis=1)
        idx_ref[:, pl.ds(i, 1), :] = tot.astype(jnp.int32)[:, None, :]
        return carry

    lax.fori_loop(0, cap // TOK_PER_ROW, slots, 0)


def _select(aff, cap):
    B, S, E = aff.shape
    R = S // TOK_PER_ROW
    packed = aff.reshape(B, R, LANES)
    idx = pl.pallas_call(
        functools.partial(_select_kernel, cap=cap, n_bisect=160),
        out_shape=jax.ShapeDtypeStruct((B, cap // TOK_PER_ROW, LANES), jnp.int32),
        grid=(1,),
        in_specs=[pl.BlockSpec((B, R, LANES), lambda i: (0, 0, 0))],
        out_specs=pl.BlockSpec((B, cap // TOK_PER_ROW, LANES), lambda i: (0, 0, 0)),
        scratch_shapes=[pltpu.VMEM((TOK_PER_ROW, B * R, LANES), F32)],
        compiler_params=_cparams(("arbitrary",)),
        name="expert_select",
    )(packed)
    return idx.reshape(B, cap, E).transpose(0, 2, 1)


def _expert_kernel(idx_ref, h2_hbm, acc_hbm, wg_ref, wu_ref, wd_ref, wr_ref, gt_ref,
                   out_hbm, xs_ref, xb_ref, y_ref, g_ref, sem_x, sem_a, sem_s,
                   *, cap, tn, nf, n_groups):
    del acc_hbm
    e = pl.program_id(0)
    hf = pl.program_id(1)
    f = pl.program_id(2)
    MT, _, D = xs_ref.shape
    M = MT * SUBLANES
    XT = MT // nf
    j = e * pl.num_programs(1) + hf
    last = n_groups - 1
    slot = j % 2

    def gather(hbm, row, dst, sem):
        pltpu.make_async_copy(hbm.at[pl.ds(row, 1)], dst, sem).start()

    def wait_tiles(n_tiles, sem):
        view = xs_ref.at[pl.ds(0, n_tiles)]
        pltpu.make_async_copy(view, view, sem).wait()

    def scatter_group(group, src_slot):
        def body(t, carry):
            for u in range(SUBLANES):
                row = idx_ref[group * M + t * SUBLANES + u]
                pltpu.make_async_copy(y_ref.at[src_slot, t, pl.ds(u, 1)],
                                      out_hbm.at[pl.ds(row, 1)], sem_s).start()
            return carry
        lax.fori_loop(0, MT, body, 0)

    @pl.when(f == 0)
    def _():
        @pl.when(j == 0)
        def _():
            def first_rows(t, carry):
                for u in range(SUBLANES):
                    row = idx_ref[t * SUBLANES + u]
                    gather(h2_hbm, row, xs_ref.at[t, pl.ds(u, 1)], sem_x)
                return carry
            lax.fori_loop(0, MT, first_rows, 0)

            def first_acc(t, carry):
                for u in range(SUBLANES):
                    row = idx_ref[t * SUBLANES + u]
                    gather(out_hbm, row, y_ref.at[0, t, pl.ds(u, 1)], sem_a)
                return carry
            lax.fori_loop(0, MT - XT, first_acc, 0)

        wait_tiles(MT, sem_x)
        xb = xs_ref[...].reshape(M, D).astype(BF16)
        xb_ref[...] = xb
        logits = jnp.dot(xb, wr_ref[...], preferred_element_type=F32)
        ex = jnp.exp(logits - jnp.max(logits, axis=-1, keepdims=True))
        aff = ex / jnp.sum(ex, axis=-1, keepdims=True)
        lane = lax.broadcasted_iota(jnp.int32, aff.shape, 1)
        g_ref[...] = jnp.sum(jnp.where(lane == e, aff, 0.0), axis=-1, keepdims=True)

        @pl.when(j >= 1)
        def _():
            scatter_group(j - 1, 1 - slot)

    @pl.when((f == 1) & (j >= 1))
    def _():
        wait_tiles(MT, sem_s)

    later = (f >= 1).astype(jnp.int32)
    x_base = jnp.minimum(j + 1, last) * M + f * (XT * SUBLANES)
    a_chunk = lax.rem(f + (nf - 1), nf)
    a_base = jnp.minimum(j + later, last) * M + a_chunk * (XT * SUBLANES)
    a_slot = lax.rem(j + later, 2)
    x_tile0 = f * XT
    a_tile0 = a_chunk * XT

    def fetch_tiles(tiles):
        for t in tiles:
            for u in range(SUBLANES):
                s = t * SUBLANES + u
                gather(h2_hbm, idx_ref[x_base + s], xs_ref.at[x_tile0 + t, pl.ds(u, 1)], sem_x)
                gather(out_hbm, idx_ref[a_base + s], y_ref.at[a_slot, a_tile0 + t, pl.ds(u, 1)], sem_a)

    phases = [list(range(XT))[p::3] for p in range(3)]

    xb = xb_ref[...]
    fetch_tiles(phases[0])
    a = jnp.dot(xb, wg_ref[0].astype(BF16), preferred_element_type=F32)
    fetch_tiles(phases[1])
    u = jnp.dot(xb, wu_ref[0].astype(BF16), preferred_element_type=F32)
    fetch_tiles(phases[2])
    hmid = ((a * jax.nn.sigmoid(a)) * u).astype(BF16)

    @pl.when(f == 0)
    def _():
        wait_tiles(MT, sem_a)

    g = g_ref[...]
    nb = gt_ref.shape[0]
    ct = cap // SUBLANES
    for n0 in range(0, D, tn):
        yc = jnp.dot(hmid, wd_ref[0, :, n0:n0 + tn].astype(BF16), preferred_element_type=F32)
        yc = yc * g
        for b in range(nb):
            contrib = gt_ref[b, :, n0:n0 + tn] * yc[b * cap:(b + 1) * cap]
            y_ref[slot, b * ct:(b + 1) * ct, :, n0:n0 + tn] += contrib.reshape(ct, SUBLANES, tn)

    @pl.when((j == last) & (f == nf - 1))
    def _():
        wait_tiles(MT, sem_x)
        wait_tiles(MT - XT, sem_a)
        scatter_group(j, slot)
        wait_tiles(MT, sem_s)


def _experts(idx_rows, h2, x1, w_gate, w_up, w_down, wr_bf, gt2, cap, n_half, tf, tn):
    N, D = h2.shape
    E, _, F = w_gate.shape
    B = gt2.shape[0]
    bh = B // n_half
    M = bh * cap
    nf = F // tf
    assert nf >= 2 and (M // SUBLANES) % nf == 0
    grid_spec = pltpu.PrefetchScalarGridSpec(
        num_scalar_prefetch=1,
        grid=(E, n_half, nf),
        in_specs=[pl.BlockSpec(memory_space=pl.ANY),
                  pl.BlockSpec(memory_space=pl.ANY),
                  pl.BlockSpec((1, D, tf), lambda e, h, f, idx: (e, 0, f)),
                  pl.BlockSpec((1, D, tf), lambda e, h, f, idx: (e, 0, f)),
                  pl.BlockSpec((1, tf, D), lambda e, h, f, idx: (e, f, 0)),
                  pl.BlockSpec((D, E), lambda e, h, f, idx: (0, 0)),
                  pl.BlockSpec((bh, 1, D), lambda e, h, f, idx: (h, 0, 0))],
        out_specs=pl.BlockSpec(memory_space=pl.ANY),
        scratch_shapes=[pltpu.VMEM((M // SUBLANES, SUBLANES, D), F32),
                        pltpu.VMEM((M, D), BF16),
                        pltpu.VMEM((2, M // SUBLANES, SUBLANES, D), F32),
                        pltpu.VMEM((M, 1), F32),
                        pltpu.SemaphoreType.DMA, pltpu.SemaphoreType.DMA, pltpu.SemaphoreType.DMA],
    )
    return pl.pallas_call(
        functools.partial(_expert_kernel, cap=cap, tn=tn, nf=nf, n_groups=E * n_half),
        out_shape=jax.ShapeDtypeStruct((N, D), F32),
        grid_spec=grid_spec,
        input_output_aliases={2: 0},
        compiler_params=_cparams(("arbitrary", "arbitrary", "arbitrary")),
        name="moe_experts",
    )(idx_rows, h2, x1, w_gate, w_up, w_down, wr_bf, gt2)


def kernel(x, c, w_ada, b_ada, g_norm1, w_in, qn_a, kn_a, qn_b, kn_b, sink_b, rel_bias,
           w_out, g_norm2, w_router, w_gate, w_up, w_down):
    B, S, D = x.shape
    E = w_router.shape[1]
    cap = CAPACITY_FACTOR * S // E

    mod = _ada_mod(c, w_ada, b_ada).reshape(B, 6, 1, D)
    sh1, sc1, gt1, sh2, sc2, gt2 = (mod[:, i] for i in range(6))

    tm = min(512, S)
    qa, ka, va, qb, kb, vb = _qkv_proj(x, g_norm1, sc1, sh1, w_in.astype(BF16),
                                       qn_a, kn_a, qn_b, kn_b, tm)
    oa = _attn_a(qa, ka, va, tq=min(256, S), tk=min(256, S // 2))
    ob = _attn_b(qb, kb, vb, rel_bias, sink_b)

    x1, h2, aff = _outproj(oa, ob, x, w_out.astype(BF16), gt1, g_norm2, sc2, sh2,
                           w_router.astype(BF16), tm)

    idx = _select(aff, cap)
    rows = idx + (jnp.arange(B, dtype=jnp.int32) * S)[:, None, None]
    rows = rows.transpose(1, 0, 2).reshape(-1)

    out = _experts(rows, h2.reshape(B * S, D), x1.reshape(B * S, D), w_gate, w_up, w_down,
                   w_router.astype(BF16), gt2, cap, n_half=2 if B % 2 == 0 else 1,
                   tf=256, tn=min(512, D))
    return out.reshape(B, S, D)
```

```python
import functools
import math

import jax
import jax.numpy as jnp
from jax import lax
from jax.experimental import pallas as pl
from jax.experimental.pallas import tpu as pltpu

HEAD_DIM = 128
A_Q_HEADS = 8
A_KV_HEADS = 2
B_Q_HEADS = 8
B_KV_HEADS = 2
GQA_GROUP = 4
BLOCK = 128
WINDOW = 128
GRID_W = 64
ROPE_THETA = 10000.0
ROPE_AXIS_DIM = HEAD_DIM // 2
N_BUCKETS = 32
MAX_DISTANCE = 128
N_EXPERTS = 16
CAPACITY_FACTOR = 2
EPS = 1e-6
NEG_BIG = -1e30
LOG2E = math.log2(math.e)

LANES = 128
TOK_PER_ROW = LANES // N_EXPERTS
VMEM_LIMIT = 56 * 1024 * 1024
SUBLANES = 8

F32 = jnp.float32
BF16 = jnp.bfloat16


def _cparams(sem):
    return pltpu.CompilerParams(dimension_semantics=sem, vmem_limit_bytes=VMEM_LIMIT)


def _ada_kernel(c_ref, w_ref, b_ref, o_ref):
    c = c_ref[...]
    s = c * jax.nn.sigmoid(c)
    o_ref[...] = jnp.dot(s.astype(BF16), w_ref[...].astype(BF16),
                         preferred_element_type=F32) + b_ref[...]


def _ada_mod(c, w_ada, b_ada):
    B, D = c.shape
    N = w_ada.shape[1]
    rows = 8
    cp = jnp.zeros((rows, D), F32).at[:B].set(c)
    tn = 1024 if N % 1024 == 0 else 512
    assert N % tn == 0
    out = pl.pallas_call(
        _ada_kernel,
        out_shape=jax.ShapeDtypeStruct((rows, N), F32),
        grid=(N // tn,),
        in_specs=[pl.BlockSpec((rows, D), lambda j: (0, 0)),
                  pl.BlockSpec((D, tn), lambda j: (0, j)),
                  pl.BlockSpec((1, tn), lambda j: (0, j))],
        out_specs=pl.BlockSpec((rows, tn), lambda j: (0, j)),
        compiler_params=_cparams(("arbitrary",)),
        name="ada_mod",
    )(cp, w_ada, b_ada.reshape(1, N))
    return out[:B]


def _head_norm(z, gain):
    ms = jnp.mean(z * z, axis=-1, keepdims=True)
    return z * lax.rsqrt(ms + EPS) * gain


def _qkv_kernel(x_ref, g1_ref, sc_ref, sh_ref, w_ref, qna_ref, kna_ref, qnb_ref, knb_ref,
                cos_ref, sin_ref, qa_ref, ka_ref, va_ref, qb_ref, kb_ref, vb_ref):
    x = x_ref[0]
    ms = jnp.mean(x * x, axis=-1, keepdims=True)
    h = (x * lax.rsqrt(ms + EPS) * g1_ref[...]) * (1.0 + sc_ref[0]) + sh_ref[0]
    z = jnp.dot(h.astype(BF16), w_ref[...], preferred_element_type=F32)

    cos = cos_ref[...]
    sin = sin_ref[...]
    lane = lax.broadcasted_iota(jnp.int32, cos.shape, 1)
    first_half = (lane % (ROPE_AXIS_DIM)) < (ROPE_AXIS_DIM // 2)
    scale = 1.0 / math.sqrt(HEAD_DIM)

    def rope(v):
        partner = jnp.where(first_half,
                            pltpu.roll(v, HEAD_DIM - ROPE_AXIS_DIM // 2, 1),
                            pltpu.roll(v, ROPE_AXIS_DIM // 2, 1))
        return v * cos + partner * sin

    def head(col):
        return z[:, col * HEAD_DIM:(col + 1) * HEAD_DIM]

    col = 0
    for hh in range(A_Q_HEADS):
        qa_ref[0, hh] = (rope(_head_norm(head(col), qna_ref[...])) * (scale * LOG2E)).astype(BF16)
        col += 1
    for hh in range(A_KV_HEADS):
        ka_ref[0, hh] = rope(_head_norm(head(col), kna_ref[...])).astype(BF16)
        col += 1
    for hh in range(A_KV_HEADS):
        va_ref[0, hh] = head(col).astype(BF16)
        col += 1
    for hh in range(B_Q_HEADS):
        qb_ref[0, hh] = (_head_norm(head(col), qnb_ref[...]) * (scale * LOG2E)).astype(BF16)
        col += 1
    for hh in range(B_KV_HEADS):
        kb_ref[0, hh] = _head_norm(head(col), knb_ref[...]).astype(BF16)
        col += 1
    for hh in range(B_KV_HEADS):
        vb_ref[0, hh] = head(col).astype(BF16)
        col += 1


def _rope_tables(S):
    rows = S // GRID_W
    row = jnp.repeat(jnp.arange(rows, dtype=F32), GRID_W)
    col = jnp.tile(jnp.arange(GRID_W, dtype=F32), rows)
    inv = 1.0 / (ROPE_THETA ** (jnp.arange(0, ROPE_AXIS_DIM, 2, dtype=F32) / ROPE_AXIS_DIM))
    ang_r = row[:, None] * inv[None, :]
    ang_c = col[:, None] * inv[None, :]
    cr, sr, cc, sc = jnp.cos(ang_r), jnp.sin(ang_r), jnp.cos(ang_c), jnp.sin(ang_c)
    cos = jnp.concatenate([cr, cr, cc, cc], axis=-1)
    sin = jnp.concatenate([-sr, sr, -sc, sc], axis=-1)
    return cos, sin


def _qkv_proj(x, g1, sc1, sh1, w_in_bf, qn_a, kn_a, qn_b, kn_b, tm):
    B, S, D = x.shape
    d_in = w_in_bf.shape[1]
    cos, sin = _rope_tables(S)
    vec = lambda v: v.reshape(1, -1).astype(F32)
    const = lambda shape: pl.BlockSpec(shape, lambda b, i: (0,) * len(shape))
    head_out = lambda nh: pl.BlockSpec((1, nh, tm, HEAD_DIM), lambda b, i: (b, 0, i, 0))
    head_shape = lambda nh: jax.ShapeDtypeStruct((B, nh, S, HEAD_DIM), BF16)
    return pl.pallas_call(
        _qkv_kernel,
        out_shape=(head_shape(A_Q_HEADS), head_shape(A_KV_HEADS), head_shape(A_KV_HEADS),
                   head_shape(B_Q_HEADS), head_shape(B_KV_HEADS), head_shape(B_KV_HEADS)),
        grid=(B, S // tm),
        in_specs=[pl.BlockSpec((1, tm, D), lambda b, i: (b, i, 0)),
                  const((1, D)),
                  pl.BlockSpec((1, 1, D), lambda b, i: (b, 0, 0)),
                  pl.BlockSpec((1, 1, D), lambda b, i: (b, 0, 0)),
                  pl.BlockSpec((D, d_in), lambda b, i: (0, 0), pipeline_mode=pl.Buffered(1)),
                  const((1, HEAD_DIM)), const((1, HEAD_DIM)), const((1, HEAD_DIM)), const((1, HEAD_DIM)),
                  pl.BlockSpec((tm, HEAD_DIM), lambda b, i: (i, 0)),
                  pl.BlockSpec((tm, HEAD_DIM), lambda b, i: (i, 0))],
        out_specs=(head_out(A_Q_HEADS), head_out(A_KV_HEADS), head_out(A_KV_HEADS),
                   head_out(B_Q_HEADS), head_out(B_KV_HEADS), head_out(B_KV_HEADS)),
        compiler_params=_cparams(("parallel", "parallel")),
        name="qkv_proj",
    )(x, vec(g1), sc1, sh1, w_in_bf, vec(qn_a), vec(kn_a), vec(qn_b), vec(kn_b), cos, sin)


def _attn_a_kernel(q_ref, k_ref, vt_ref, o_ref, s_ref, mx_ref, acc_ref, m_ref, l_ref, *, tk):
    G, tq, d = q_ref.shape[1:]
    S = k_ref.shape[2]
    R = G * tq
    nk = S // tk
    q = q_ref[0].reshape(R, d)

    def scores(slot, j):
        k0 = pl.multiple_of(j * tk, tk)
        k = k_ref[0, 0, pl.ds(k0, tk), :]
        s = lax.dot_general(k, q, (((1,), (1,)), ((), ())), preferred_element_type=F32)
        s_ref[slot] = s
        mx_ref[slot] = jnp.max(s, axis=0, keepdims=True)

    def update(slot, j):
        k0 = pl.multiple_of(j * tk, tk)
        s = s_ref[slot]
        m = m_ref[...]
        m_new = jnp.maximum(m, mx_ref[slot])
        alpha = jnp.exp2(m - m_new)
        p = jnp.exp2(s - m_new)
        l_ref[...] = alpha * l_ref[...] + jnp.sum(p, axis=0, keepdims=True)
        vt = vt_ref[0, 0, :, pl.ds(k0, tk)]
        acc_ref[...] = alpha * acc_ref[...] + jnp.dot(vt, p.astype(BF16),
                                                      preferred_element_type=F32)
        m_ref[...] = m_new

    m_ref[...] = jnp.full((1, R), -jnp.inf, F32)
    l_ref[...] = jnp.zeros((1, R), F32)
    acc_ref[...] = jnp.zeros((d, R), F32)
    scores(0, 0)

    def body(jj, carry):
        j = 2 * jj
        scores(1, j + 1)
        update(0, j)
        scores(0, jnp.minimum(j + 2, nk - 1))
        update(1, j + 1)
        return carry

    lax.fori_loop(0, nk // 2, body, 0)
    out = (acc_ref[...] / l_ref[...]).T.astype(o_ref.dtype)
    for g in range(G):
        o_ref[0, :, g * d:(g + 1) * d] = out[g * tq:(g + 1) * tq]


def _attn_a(q, k, v, tq, tk):
    B, Hq, S, d = q.shape
    Hkv = k.shape[1]
    G = Hq // Hkv
    vt = v.transpose(0, 1, 3, 2)
    R = G * tq
    assert (S // tk) % 2 == 0
    return pl.pallas_call(
        functools.partial(_attn_a_kernel, tk=tk),
        out_shape=jax.ShapeDtypeStruct((B, S, Hq * d), BF16),
        grid=(B, Hkv, S // tq),
        in_specs=[pl.BlockSpec((1, G, tq, d), lambda b, h, i: (b, h, i, 0)),
                  pl.BlockSpec((1, 1, S, d), lambda b, h, i: (b, h, 0, 0)),
                  pl.BlockSpec((1, 1, d, S), lambda b, h, i: (b, h, 0, 0))],
        out_specs=pl.BlockSpec((1, tq, G * d), lambda b, h, i: (b, i, h)),
        scratch_shapes=[pltpu.VMEM((2, tk, R), F32), pltpu.VMEM((2, 1, R), F32), pltpu.VMEM((d, R), F32),
                        pltpu.VMEM((1, R), F32), pltpu.VMEM((1, R), F32)],
        compiler_params=_cparams(("parallel", "parallel", "arbitrary")),
        name="attn_global",
    )(q, k, vt)


def _attn_b_kernel(rb_ref, sink_ref, q_ref, k_ref, vt_ref, bucket_ref, o_ref,
                   bias_ref, sink_row_ref, s_ref, mx_ref):
    G, S, d = q_ref.shape[1:]
    kw = 3 * BLOCK
    R = G * BLOCK
    nblk = S // BLOCK
    h = pl.program_id(0)

    @pl.when(pl.program_id(1) == 0)
    def _():
        bucket = bucket_ref[...]
        lane_head = lax.broadcasted_iota(jnp.int32, (1, R), 1) // BLOCK
        sink_row = jnp.zeros((1, R), F32)
        for g in range(G):
            tab = jnp.full(bucket.shape, NEG_BIG, F32)
            for kb in range(N_BUCKETS):
                tab = jnp.where(bucket == kb, rb_ref[kb, h * G + g] * LOG2E, tab)
            bias_ref[:, g * BLOCK:(g + 1) * BLOCK] = tab
            sink_row = jnp.where(lane_head == g, sink_ref[h * G + g] * LOG2E, sink_row)
        sink_row_ref[...] = sink_row

    def rows(n):
        r0 = n * BLOCK
        return r0 if isinstance(n, int) else pl.multiple_of(r0, BLOCK)

    def scores(n, lo, hi):
        r0 = rows(n)
        q = q_ref[0, :, pl.ds(r0, BLOCK), :].reshape(R, d)
        k = k_ref[0, 0, pl.ds(r0 + (lo - BLOCK), hi - lo), :]
        s = lax.dot_general(k, q, (((1,), (1,)), ((), ())), preferred_element_type=F32)
        return s + bias_ref[lo:hi, :]

    def finish(n, lo, hi, s, smax):
        r0 = rows(n)
        sink = sink_row_ref[...]
        m = jnp.maximum(smax, sink)
        p = jnp.exp2(s - m)
        denom = jnp.sum(p, axis=0, keepdims=True) + jnp.exp2(sink - m)
        vt = vt_ref[0, 0, :, pl.ds(r0 + (lo - BLOCK), hi - lo)]
        ot = jnp.dot(vt, p.astype(BF16), preferred_element_type=F32)
        o = (ot / denom).T.astype(o_ref.dtype)
        for g in range(G):
            o_ref[0, pl.ds(r0, BLOCK), g * d:(g + 1) * d] = o[g * BLOCK:(g + 1) * BLOCK]

    def edge(n, lo, hi):
        s = scores(n, lo, hi)
        finish(n, lo, hi, s, jnp.max(s, axis=0, keepdims=True))

    def stage(slot, n):
        s = scores(n, 0, kw)
        s_ref[slot] = s
        mx_ref[slot] = jnp.max(s, axis=0, keepdims=True)

    edge(0, BLOCK, kw)
    stage(0, 1)

    def body(i, carry):
        n = 1 + 2 * i
        stage(1, n + 1)
        finish(n, 0, kw, s_ref[0], mx_ref[0])
        stage(0, jnp.minimum(n + 2, nblk - 2))
        finish(n + 1, 0, kw, s_ref[1], mx_ref[1])
        return carry

    lax.fori_loop(0, (nblk - 2) // 2, body, 0)
    edge(nblk - 1, 0, 2 * BLOCK)


def _t5_bucket(rel):
    nb = N_BUCKETS // 2
    ret = jnp.where(rel > 0, nb, 0)
    n = jnp.abs(rel)
    max_exact = nb // 2
    nf = jnp.maximum(n, 1).astype(F32)
    large = max_exact + (jnp.log(nf / max_exact) / math.log(MAX_DISTANCE / max_exact)
                         * (nb - max_exact)).astype(jnp.int32)
    large = jnp.minimum(large, nb - 1)
    return ret + jnp.where(n < max_exact, n, large)


def _window_buckets():
    m = jnp.arange(3 * BLOCK)
    j = jnp.arange(BLOCK)
    rel = m[:, None] - BLOCK - j[None, :]
    return jnp.where(jnp.abs(rel) <= WINDOW, _t5_bucket(rel), -1).astype(jnp.int32)


def _attn_b(q, k, v, rel_bias, sink):
    B, Hq, S, d = q.shape
    Hkv = k.shape[1]
    G = Hq // Hkv
    n_inner = S // BLOCK - 2
    assert n_inner >= 2 and n_inner % 2 == 0
    vt = v.transpose(0, 1, 3, 2)
    kw, R = 3 * BLOCK, G * BLOCK
    smem = pl.BlockSpec(memory_space=pltpu.SMEM)
    return pl.pallas_call(
        _attn_b_kernel,
        out_shape=jax.ShapeDtypeStruct((B, S, Hq * d), BF16),
        grid=(Hkv, B),
        in_specs=[smem, smem,
                  pl.BlockSpec((1, G, S, d), lambda h, b: (b, h, 0, 0)),
                  pl.BlockSpec((1, 1, S, d), lambda h, b: (b, h, 0, 0)),
                  pl.BlockSpec((1, 1, d, S), lambda h, b: (b, h, 0, 0)),
                  pl.BlockSpec((kw, BLOCK), lambda h, b: (0, 0))],
        out_specs=pl.BlockSpec((1, S, G * d), lambda h, b: (b, 0, h)),
        scratch_shapes=[pltpu.VMEM((kw, R), F32), pltpu.VMEM((1, R), F32),
                        pltpu.VMEM((2, kw, R), F32), pltpu.VMEM((2, 1, R), F32)],
        compiler_params=_cparams(("arbitrary", "arbitrary")),
        name="attn_window",
    )(rel_bias.astype(F32), sink.astype(F32), q, k, vt, _window_buckets())


def _outproj_kernel(oa_ref, ob_ref, x_ref, w_ref, gt_ref, g2_ref, sc_ref, sh_ref, wr_ref,
                    x1_ref, h2_ref, aff_ref):
    half = oa_ref.shape[2]
    mix = jnp.dot(oa_ref[0], w_ref[:half, :], preferred_element_type=F32)
    mix = mix + jnp.dot(ob_ref[0], w_ref[half:, :], preferred_element_type=F32)
    x1 = x_ref[0] + gt_ref[0] * mix
    x1_ref[0] = x1
    ms = jnp.mean(x1 * x1, axis=-1, keepdims=True)
    h2 = (x1 * lax.rsqrt(ms + EPS) * g2_ref[...]) * (1.0 + sc_ref[0]) + sh_ref[0]
    h2_ref[0] = h2
    logits = jnp.dot(h2.astype(BF16), wr_ref[...], preferred_element_type=F32)
    e = jnp.exp(logits - jnp.max(logits, axis=-1, keepdims=True))
    aff_ref[0] = e / jnp.sum(e, axis=-1, keepdims=True)


def _outproj(oa, ob, x, w_out_bf, gt1, g2, sc2, sh2, wr_bf, tm):
    B, S, D = x.shape
    half = oa.shape[2]
    E = wr_bf.shape[1]
    const = lambda shape: pl.BlockSpec(shape, lambda b, i: (0,) * len(shape))
    per_b = pl.BlockSpec((1, 1, D), lambda b, i: (b, 0, 0))
    return pl.pallas_call(
        _outproj_kernel,
        out_shape=(jax.ShapeDtypeStruct((B, S, D), F32),
                   jax.ShapeDtypeStruct((B, S, D), F32),
                   jax.ShapeDtypeStruct((B, S, E), F32)),
        grid=(B, S // tm),
        in_specs=[pl.BlockSpec((1, tm, half), lambda b, i: (b, i, 0)),
                  pl.BlockSpec((1, tm, half), lambda b, i: (b, i, 0)),
                  pl.BlockSpec((1, tm, D), lambda b, i: (b, i, 0)),
                  pl.BlockSpec((2 * half, D), lambda b, i: (0, 0), pipeline_mode=pl.Buffered(1)),
                  per_b, const((1, D)), per_b, per_b,
                  const((D, E))],
        out_specs=(pl.BlockSpec((1, tm, D), lambda b, i: (b, i, 0)),
                   pl.BlockSpec((1, tm, D), lambda b, i: (b, i, 0)),
                   pl.BlockSpec((1, tm, E), lambda b, i: (b, i, 0))),
        compiler_params=_cparams(("parallel", "parallel")),
        name="outproj_router",
    )(oa, ob, x, w_out_bf, gt1, g2.reshape(1, D).astype(F32), sc2, sh2, wr_bf)


def _group_allreduce(v, op):
    shift = N_EXPERTS
    while shift < LANES:
        v = op(v, pltpu.roll(v, shift, 1))
        shift *= 2
    return v


def _select_kernel(aff_ref, idx_ref, rot_ref, *, cap, n_bisect):
    B, R, L = aff_ref.shape
    a = aff_ref[...]
    capf = jnp.float32(cap)

    def count(mask):
        return _group_allreduce(jnp.sum(jnp.where(mask, 1.0, 0.0), axis=1), jnp.add)

    def bis(_, carry):
        lo, hi = carry
        mid = 0.5 * (lo + hi)
        ge = count(a > mid[:, None, :]) >= capf
        return jnp.where(ge, mid, lo), jnp.where(ge, hi, mid)

    lo0 = jnp.full((B, L), -1.0, F32)
    hi0 = jnp.full((B, L), 1.0, F32)
    lo, _ = lax.fori_loop(0, n_bisect, bis, (lo0, hi0))
    tau = jnp.min(jnp.where(a > lo[:, None, :], a, jnp.inf), axis=1)
    tau = _group_allreduce(tau, jnp.minimum)[:, None, :]

    lane_r = lax.broadcasted_iota(jnp.int32, (L, L), 0)
    lane_c = lax.broadcasted_iota(jnp.int32, (L, L), 1)
    same_e = (lane_r % N_EXPERTS) == (lane_c % N_EXPERTS)
    g_incl = jnp.where(same_e & (lane_r // N_EXPERTS <= lane_c // N_EXPERTS), 1.0, 0.0).astype(BF16)
    g_full = jnp.where(same_e, 1.0, 0.0).astype(BF16)
    row_r = lax.broadcasted_iota(jnp.int32, (R, R), 0)
    row_c = lax.broadcasted_iota(jnp.int32, (R, R), 1)
    l_strict = jnp.where(row_c < row_r, 1.0, 0.0).astype(BF16)

    def prefix_incl(mask):
        ones = jnp.where(mask, 1.0, 0.0)
        outs = []
        for b in range(B):
            mb = ones[b].astype(BF16)
            within = jnp.dot(mb, g_incl, preferred_element_type=F32)
            tot = jnp.dot(mb, g_full, preferred_element_type=F32)
            before = jnp.dot(l_strict, tot.astype(BF16), preferred_element_type=F32)
            outs.append((within + before)[None])
        return jnp.concatenate(outs, axis=0)

    gt = a > tau
    eq = a == tau
    need = (capf - count(gt))[:, None, :]
    eq_before = prefix_incl(eq) - jnp.where(eq, 1.0, 0.0)
    sel = gt | (eq & (eq_before < need))
    rank = prefix_incl(sel)

    rank2 = rank.reshape(B * R, L)
    for rho in range(TOK_PER_ROW):
        rot_ref[rho] = rank2 if rho == 0 else pltpu.roll(rank2, rho * N_EXPERTS, 1)
    slot_in_row = (lax.broadcasted_iota(jnp.int32, (1, L), 1) // N_EXPERTS).astype(F32)

    def slots(i, carry):
        cvec = slot_in_row + jnp.asarray(i * TOK_PER_ROW, F32)
        tot = jnp.zeros((B, L), F32)
        for rho in range(TOK_PER_ROW):
            le = jnp.where(rot_ref[rho] <= cvec, 1.0, 0.0)
            tot = tot + jnp.sum(le.reshape(B, R, L), axis=1)
        idx_ref[:, pl.ds(i, 1), :] = tot.astype(jnp.int32)[:, None, :]
        return carry

    lax.fori_loop(0, cap // TOK_PER_ROW, slots, 0)


def _select(aff, cap):
    B, S, E = aff.shape
    R = S // TOK_PER_ROW
    packed = aff.reshape(B, R, LANES)
    idx = pl.pallas_call(
        functools.partial(_select_kernel, cap=cap, n_bisect=160),
        out_shape=jax.ShapeDtypeStruct((B, cap // TOK_PER_ROW, LANES), jnp.int32),
        grid=(1,),
        in_specs=[pl.BlockSpec((B, R, LANES), lambda i: (0, 0, 0))],
        out_specs=pl.BlockSpec((B, cap // TOK_PER_ROW, LANES), lambda i: (0, 0, 0)),
        scratch_shapes=[pltpu.VMEM((TOK_PER_ROW, B * R, LANES), F32)],
        compiler_params=_cparams(("arbitrary",)),
        name="expert_select",
    )(packed)
    return idx.reshape(B, cap, E).transpose(0, 2, 1)


def _expert_kernel(idx_ref, h2_hbm, acc_hbm, wg_ref, wu_ref, wd_ref, wr_ref, gt_ref,
                   out_hbm, xs_ref, xb_ref, y_ref, g_ref, sem_x, sem_a, sem_s,
                   *, cap, tn, nf, n_groups):
    del acc_hbm
    e = pl.program_id(0)
    hf = pl.program_id(1)
    f = pl.program_id(2)
    MT, _, D = xs_ref.shape
    M = MT * SUBLANES
    XT = MT // nf
    j = e * pl.num_programs(1) + hf
    last = n_groups - 1
    slot = j % 2

    def gather(hbm, row, dst, sem):
        pltpu.make_async_copy(hbm.at[pl.ds(row, 1)], dst, sem).start()

    def wait_tiles(n_tiles, sem):
        view = xs_ref.at[pl.ds(0, n_tiles)]
        pltpu.make_async_copy(view, view, sem).wait()

    def scatter_group(group, src_slot):
        def body(t, carry):
            for u in range(SUBLANES):
                row = idx_ref[group * M + t * SUBLANES + u]
                pltpu.make_async_copy(y_ref.at[src_slot, t, pl.ds(u, 1)],
                                      out_hbm.at[pl.ds(row, 1)], sem_s).start()
            return carry
        lax.fori_loop(0, MT, body, 0)

    @pl.when(f == 0)
    def _():
        @pl.when(j == 0)
        def _():
            def first_rows(t, carry):
                for u in range(SUBLANES):
                    row = idx_ref[t * SUBLANES + u]
                    gather(h2_hbm, row, xs_ref.at[t, pl.ds(u, 1)], sem_x)
                return carry
            lax.fori_loop(0, MT, first_rows, 0)

            def first_acc(t, carry):
                for u in range(SUBLANES):
                    row = idx_ref[t * SUBLANES + u]
                    gather(out_hbm, row, y_ref.at[0, t, pl.ds(u, 1)], sem_a)
                return carry
            lax.fori_loop(0, MT - XT, first_acc, 0)

        wait_tiles(MT, sem_x)
        xb = xs_ref[...].reshape(M, D).astype(BF16)
        xb_ref[...] = xb
        logits = jnp.dot(xb, wr_ref[...], preferred_element_type=F32)
        ex = jnp.exp(logits - jnp.max(logits, axis=-1, keepdims=True))
        aff = ex / jnp.sum(ex, axis=-1, keepdims=True)
        lane = lax.broadcasted_iota(jnp.int32, aff.shape, 1)
        g_ref[...] = jnp.sum(jnp.where(lane == e, aff, 0.0), axis=-1, keepdims=True)

        @pl.when(j >= 1)
        def _():
            scatter_group(j - 1, 1 - slot)

    @pl.when((f == 1) & (j >= 1))
    def _():
        wait_tiles(MT, sem_s)

    later = (f >= 1).astype(jnp.int32)
    x_base = jnp.minimum(j + 1, last) * M + f * (XT * SUBLANES)
    a_chunk = lax.rem(f + (nf - 1), nf)
    a_base = jnp.minimum(j + later, last) * M + a_chunk * (XT * SUBLANES)
    a_slot = lax.rem(j + later, 2)
    x_tile0 = f * XT
    a_tile0 = a_chunk * XT

    def fetch_tiles(tiles):
        for t in tiles:
            for u in range(SUBLANES):
                s = t * SUBLANES + u
                gather(h2_hbm, idx_ref[x_base + s], xs_ref.at[x_tile0 + t, pl.ds(u, 1)], sem_x)
                gather(out_hbm, idx_ref[a_base + s], y_ref.at[a_slot, a_tile0 + t, pl.ds(u, 1)], sem_a)

    phases = [list(range(XT))[p::3] for p in range(3)]

    xb = xb_ref[...]
    fetch_tiles(phases[0])
    a = jnp.dot(xb, wg_ref[0].astype(BF16), preferred_element_type=F32)
    fetch_tiles(phases[1])
    u = jnp.dot(xb, wu_ref[0].astype(BF16), preferred_element_type=F32)
    fetch_tiles(phases[2])
    hmid = ((a * jax.nn.sigmoid(a)) * u).astype(BF16)

    @pl.when(f == 0)
    def _():
        wait_tiles(MT, sem_a)

    g = g_ref[...]
    nb = gt_ref.shape[0]
    ct = cap // SUBLANES
    for n0 in range(0, D, tn):
        yc = jnp.dot(hmid, wd_ref[0, :, n0:n0 + tn].astype(BF16), preferred_element_type=F32)
        yc = yc * g
        for b in range(nb):
            contrib = gt_ref[b, :, n0:n0 + tn] * yc[b * cap:(b + 1) * cap]
            y_ref[slot, b * ct:(b + 1) * ct, :, n0:n0 + tn] += contrib.reshape(ct, SUBLANES, tn)

    @pl.when((j == last) & (f == nf - 1))
    def _():
        wait_tiles(MT, sem_x)
        wait_tiles(MT - XT, sem_a)
        scatter_group(j, slot)
        wait_tiles(MT, sem_s)


def _experts(idx_rows, h2, x1, w_gate, w_up, w_down, wr_bf, gt2, cap, n_half, tf, tn):
    N, D = h2.shape
    E, _, F = w_gate.shape
    B = gt2.shape[0]
    bh = B // n_half
    M = bh * cap
    nf = F // tf
    assert nf >= 2 and (M // SUBLANES) % nf == 0
    grid_spec = pltpu.PrefetchScalarGridSpec(
        num_scalar_prefetch=1,
        grid=(E, n_half, nf),
        in_specs=[pl.BlockSpec(memory_space=pl.ANY),
                  pl.BlockSpec(memory_space=pl.ANY),
                  pl.BlockSpec((1, D, tf), lambda e, h, f, idx: (e, 0, f)),
                  pl.BlockSpec((1, D, tf), lambda e, h, f, idx: (e, 0, f)),
                  pl.BlockSpec((1, tf, D), lambda e, h, f, idx: (e, f, 0)),
                  pl.BlockSpec((D, E), lambda e, h, f, idx: (0, 0)),
                  pl.BlockSpec((bh, 1, D), lambda e, h, f, idx: (h, 0, 0))],
        out_specs=pl.BlockSpec(memory_space=pl.ANY),
        scratch_shapes=[pltpu.VMEM((M // SUBLANES, SUBLANES, D), F32),
                        pltpu.VMEM((M, D), BF16),
                        pltpu.VMEM((2, M // SUBLANES, SUBLANES, D), F32),
                        pltpu.VMEM((M, 1), F32),
                        pltpu.SemaphoreType.DMA, pltpu.SemaphoreType.DMA, pltpu.SemaphoreType.DMA],
    )
    return pl.pallas_call(
        functools.partial(_expert_kernel, cap=cap, tn=tn, nf=nf, n_groups=E * n_half),
        out_shape=jax.ShapeDtypeStruct((N, D), F32),
        grid_spec=grid_spec,
        input_output_aliases={2: 0},
        compiler_params=_cparams(("arbitrary", "arbitrary", "arbitrary")),
        name="moe_experts",
    )(idx_rows, h2, x1, w_gate, w_up, w_down, wr_bf, gt2)


def kernel(x, c, w_ada, b_ada, g_norm1, w_in, qn_a, kn_a, qn_b, kn_b, sink_b, rel_bias,
           w_out, g_norm2, w_router, w_gate, w_up, w_down):
    B, S, D = x.shape
    E = w_router.shape[1]
    cap = CAPACITY_FACTOR * S // E

    mod = _ada_mod(c, w_ada, b_ada).reshape(B, 6, 1, D)
    sh1, sc1, gt1, sh2, sc2, gt2 = (mod[:, i] for i in range(6))

    tm = min(512, S)
    qa, ka, va, qb, kb, vb = _qkv_proj(x, g_norm1, sc1, sh1, w_in.astype(BF16),
                                       qn_a, kn_a, qn_b, kn_b, tm)
    oa = _attn_a(qa, ka, va, tq=min(256, S), tk=min(512, S // 2))
    ob = _attn_b(qb, kb, vb, rel_bias, sink_b)

    x1, h2, aff = _outproj(oa, ob, x, w_out.astype(BF16), gt1, g_norm2, sc2, sh2,
                           w_router.astype(BF16), tm)

    idx = _select(aff, cap)
    rows = idx + (jnp.arange(B, dtype=jnp.int32) * S)[:, None, None]
    rows = rows.transpose(1, 0, 2).reshape(-1)

    out = _experts(rows, h2.reshape(B * S, D), x1.reshape(B * S, D), w_gate, w_up, w_down,
                   w_router.astype(BF16), gt2, cap, n_half=2 if B % 2 == 0 else 1,
                   tf=256, tn=min(512, D))
    return out.reshape(B, S, D)
```
